```python
import jax, jax.numpy as jnp
from jax import lax
import numpy as np

D_MODEL = 1024
BATCH = 8
SEQ = 2048
DEPTH = 1
DEC_BATCH = 128
DEC_SEQ = 4
PAST_LEN = 16384
PAGE_SIZE = 128

D_CONV = D_MODEL
CONV_A_W = 3
N_HEADS = 8
HEAD_K = 128
HEAD_V = 128
KEY_W = N_HEADS * HEAD_K
VAL_W = N_HEADS * HEAD_V
QKV_W = 2 * KEY_W + VAL_W
CONV_B_W = 4
CHUNK = 64
EPS = 1e-6
SPLITS = (D_CONV, D_CONV, D_CONV, D_CONV, QKV_W, VAL_W, N_HEADS, N_HEADS, D_MODEL, D_MODEL)
N_IN = sum(SPLITS)

kernel_name = 'hybrid_shortconv_gdn_parallel_step'


def _split_points():
    pts, s = [], 0
    for w in SPLITS[:-1]:
        s += w
        pts.append(s)
    return pts


def _rmsnorm(x, w):
    xf = x.astype(jnp.float32)
    y = xf * lax.rsqrt(jnp.mean(xf * xf, axis=-1, keepdims=True) + EPS) * w.astype(jnp.float32)
    return y.astype(x.dtype)


def _l2norm(x):
    return x * lax.rsqrt(jnp.sum(x * x, axis=-1, keepdims=True) + EPS)


def _causal_conv(u, past, w):
    W = w.shape[0]
    T = u.shape[1]
    ext = jnp.concatenate([past.astype(u.dtype), u], axis=1)
    out = ext[:, 0:T] * w[0]
    for j in range(1, W):
        out = out + ext[:, j:j + T] * w[j]
    return out, ext[:, ext.shape[1] - (W - 1):]


def _gated_delta(q, k, v, beta, g, S0):
    Bn, T, H, K = q.shape
    V = v.shape[-1]
    C = min(CHUNK, T)
    pad = (-T) % C
    if pad:
        pw = ((0, 0), (0, pad), (0, 0), (0, 0))
        q, k, v = jnp.pad(q, pw), jnp.pad(k, pw), jnp.pad(v, pw)
        beta = jnp.pad(beta, pw[:3])
        g = jnp.pad(g, pw[:3])
    Tp = T + pad
    N = Tp // C
    def chunks(a):
        a = a.reshape((Bn, N, C) + a.shape[2:])
        return jnp.moveaxis(a, 3, 1)
    q, k, v, beta, g = chunks(q), chunks(k), chunks(v), chunks(beta), chunks(g)
    gc = jnp.cumsum(g, axis=-1)
    idx = jnp.arange(C)
    causal = idx[:, None] >= idx[None, :]
    strict = idx[:, None] > idx[None, :]
    L = jnp.exp(jnp.where(causal, gc[..., :, None] - gc[..., None, :], -jnp.inf))
    kb = k * beta[..., None]
    vb = v * beta[..., None]
    M = jnp.where(strict, jnp.einsum('bhnik,bhnjk->bhnij', kb, k) * L, 0.0)
    eye = jnp.broadcast_to(jnp.eye(C, dtype=M.dtype), M.shape)
    Tm = lax.linalg.triangular_solve(eye + M, eye, left_side=True, lower=True)
    u_pre = jnp.einsum('bhnij,bhnjv->bhniv', Tm, vb)
    w_dec = jnp.einsum('bhnij,bhnjk->bhnik', Tm, kb * jnp.exp(gc)[..., None])
    a_qk = jnp.where(causal, jnp.einsum('bhnik,bhnjk->bhnij', q, k) * L, 0.0)
    q_dec = q * jnp.exp(gc)[..., None]
    g_last = gc[..., -1]
    k_dec = k * jnp.exp(g_last[..., None] - gc)[..., None]
    xs = tuple(jnp.moveaxis(a, 2, 0) for a in (q_dec, a_qk, k_dec, u_pre, w_dec, jnp.exp(g_last)))

    def step(S, inp):
        qd, aqk, kd, up, wd, dl = inp
        u = up - jnp.einsum('bhck,bhkv->bhcv', wd, S)
        o = jnp.einsum('bhck,bhkv->bhcv', qd, S) + jnp.einsum('bhij,bhjv->bhiv', aqk, u)
        S = S * dl[..., None, None] + jnp.einsum('bhck,bhcv->bhkv', kd, u)
        return S, o

    S_new, o = lax.scan(step, S0, xs)
    o = jnp.transpose(o, (1, 0, 3, 2, 4)).reshape(Bn, Tp, H, V)[:, :T]
    return o, S_new


def _layer(x, conv_a_buf, conv_qkv_buf, S0, w_in, conv_a_w, conv_b_w, a_log, dt_bias,
           onorm_w, w_out_a, w_out_b, w_o, norm_w):
    Bn, T, _ = x.shape
    u = _rmsnorm(x, norm_w)
    proj = jnp.einsum('btd,dn->btn', u, w_in)
    a_b, a_c, a_h, a_z, qkv, b_z, b_beta, b_alpha, g_a, g_b = jnp.split(proj, _split_points(), axis=-1)
    conv_out, new_a_buf = _causal_conv(a_c * a_h, conv_a_buf, conv_a_w)
    y_a = jnp.einsum('btc,cd->btd', jax.nn.silu(a_z) * a_b * conv_out, w_out_a)
    qkv_c, new_qkv_buf = _causal_conv(qkv, conv_qkv_buf, conv_b_w)
    qkv_c = jax.nn.silu(qkv_c).astype(jnp.float32)
    q = qkv_c[..., :KEY_W].reshape(Bn, T, N_HEADS, HEAD_K)
    k = qkv_c[..., KEY_W:2 * KEY_W].reshape(Bn, T, N_HEADS, HEAD_K)
    v = qkv_c[..., 2 * KEY_W:].reshape(Bn, T, N_HEADS, HEAD_V)
    q = _l2norm(q) * (HEAD_K ** -0.5)
    k = _l2norm(k)
    beta = jax.nn.sigmoid(b_beta.astype(jnp.float32))
    g = -jnp.exp(a_log.astype(jnp.float32)) * jax.nn.softplus(b_alpha.astype(jnp.float32) + dt_bias.astype(jnp.float32))
    o, S_new = _gated_delta(q, k, v, beta, g, S0.astype(jnp.float32))
    o = _rmsnorm(o, onorm_w) * jax.nn.silu(b_z.astype(jnp.float32).reshape(Bn, T, N_HEADS, HEAD_V))
    y_b = jnp.einsum('btc,cd->btd', o.reshape(Bn, T, VAL_W).astype(x.dtype), w_out_b)
    m = jax.nn.sigmoid(g_a) * y_a + jax.nn.sigmoid(g_b) * y_b
    return x + jnp.einsum('btd,de->bte', m, w_o), new_a_buf, new_qkv_buf, S_new


def setup_inputs(seed: int = 0) -> dict:
    key = jax.random.key(seed)
    ks = jax.random.split(key, 20)
    f = jnp.float32
    nrm = lambda k, s: jax.random.normal(k, s, f)
    A = jax.random.uniform(ks[8], (DEPTH, N_HEADS), f, 1.0, 16.0)
    dt = jnp.exp(jax.random.uniform(ks[9], (DEPTH, N_HEADS), f, np.log(1e-3), np.log(1e-1)))
    return {
        'x_prompt': nrm(ks[0], (BATCH, SEQ, D_MODEL)),
        'x_sample': nrm(ks[1], (DEC_BATCH, DEC_SEQ, D_MODEL)),
        'state_conv_a': nrm(ks[2], (DEPTH, DEC_BATCH, CONV_A_W - 1, D_CONV)),
        'state_conv_qkv': nrm(ks[3], (DEPTH, DEC_BATCH, CONV_B_W - 1, QKV_W)),
        'state_delta': 0.1 * nrm(ks[4], (DEPTH, DEC_BATCH, N_HEADS, HEAD_K, HEAD_V)),
        'w_in': nrm(ks[5], (DEPTH, D_MODEL, N_IN)) * D_MODEL ** -0.5,
        'conv_a_w': nrm(ks[6], (DEPTH, CONV_A_W, D_CONV)) * CONV_A_W ** -0.5,
        'conv_b_w': nrm(ks[7], (DEPTH, CONV_B_W, QKV_W)) * CONV_B_W ** -0.5,
        'a_log': jnp.log(A),
        'dt_bias': dt + jnp.log(-jnp.expm1(-dt)),
        'onorm_w': 1.0 + 0.02 * nrm(ks[10], (DEPTH, HEAD_V)),
        'w_out_a': nrm(ks[11], (DEPTH, D_CONV, D_MODEL)) * D_CONV ** -0.5,
        'w_out_b': nrm(ks[12], (DEPTH, VAL_W, D_MODEL)) * VAL_W ** -0.5,
        'w_o': nrm(ks[13], (DEPTH, D_MODEL, D_MODEL)) * D_MODEL ** -0.5,
        'norm_w': 1.0 + 0.02 * nrm(ks[14], (DEPTH, D_MODEL)),
        'final_norm_w': 1.0 + 0.02 * nrm(ks[15], (D_MODEL,)),
    }


def reference(x_prompt, x_sample, state_conv_a, state_conv_qkv, state_delta, w_in, conv_a_w,
              conv_b_w, a_log, dt_bias, onorm_w, w_out_a, w_out_b, w_o, norm_w, final_norm_w):
    hp, hs = x_prompt, x_sample
    pa, pq, pd, sa, sq, sd = [], [], [], [], [], []
    for l in range(DEPTH):
        params = (w_in[l], conv_a_w[l], conv_b_w[l], a_log[l], dt_bias[l], onorm_w[l],
                  w_out_a[l], w_out_b[l], w_o[l], norm_w[l])
        z_a = jnp.zeros((BATCH, CONV_A_W - 1, D_CONV), hp.dtype)
        z_q = jnp.zeros((BATCH, CONV_B_W - 1, QKV_W), hp.dtype)
        z_s = jnp.zeros((BATCH, N_HEADS, HEAD_K, HEAD_V), jnp.float32)
        hp, ba, bq, bs = _layer(hp, z_a, z_q, z_s, *params)
        pa.append(ba); pq.append(bq); pd.append(bs)
        hs, ba, bq, bs = _layer(hs, state_conv_a[l], state_conv_qkv[l], state_delta[l], *params)
        sa.append(ba); sq.append(bq); sd.append(bs)
    y_prompt = _rmsnorm(hp, final_norm_w)
    y_sample = _rmsnorm(hs, final_norm_w)
    new_conv_a_prompt = jnp.stack(pa)
    new_conv_qkv_prompt = jnp.stack(pq)
    new_delta_prompt = jnp.stack(pd)
    new_conv_a_sample = jnp.stack(sa)
    new_conv_qkv_sample = jnp.stack(sq)
    new_delta_sample = jnp.stack(sd)
    return (y_prompt, y_sample, new_conv_a_prompt, new_conv_qkv_prompt, new_delta_prompt,
            new_conv_a_sample, new_conv_qkv_sample, new_delta_sample)
```

```python
import functools
import math

import jax
import jax.numpy as jnp
from jax import lax
from jax.experimental import pallas as pl
from jax.experimental.pallas import tpu as pltpu

D_MODEL = 1024
N_HEADS = 8
HEAD_DIM = 128
QKV_W = 3 * D_MODEL
CONV_A_W = 3
CONV_B_W = 4
PROMPT_CHUNK = 64
EPS = 1e-6
N_MAIN = 10 * D_MODEL
NARROW = 128
OFF_B, OFF_C, OFF_H, OFF_Z, OFF_QKV, OFF_ZB, OFF_GA, OFF_GB = (
    0, 1024, 2048, 3072, 4096, 7168, 8192, 9216)

V7X_VMEM_LIMIT = 56 * 1024 * 1024
COL_CHUNK = 256

F32 = jnp.float32
BF16 = jnp.bfloat16
HI = lax.Precision.HIGHEST


def _dot(a, b):
    return jnp.dot(a, b, preferred_element_type=F32)


def _dot_hi(a, b):
    return jnp.dot(a, b, preferred_element_type=F32, precision=HI)


def _silu(x):
    return x * jax.nn.sigmoid(x)


def _softplus(x):
    return jnp.maximum(x, 0.0) + jnp.log(1.0 + jnp.exp(-jnp.abs(x)))


def _front_kernel(tm, shift, tiles_per_seq, has_state, *refs):
    if has_state:
        (x_ref, nw_ref, wm_ref, wn_ref, caw_ref, cbw_ref, hp_ref, woa_ref, sta_ref, stq_ref,
         ma_ref, q_ref, k_ref, v_ref, zb_ref, sgb_ref, nar_ref, sa_ref, sq_ref,
         xn_s, exta_s, extb_s, pre_s) = refs
    else:
        (x_ref, nw_ref, wm_ref, wn_ref, caw_ref, cbw_ref, hp_ref, woa_ref,
         ma_ref, q_ref, k_ref, v_ref, zb_ref, sgb_ref, nar_ref, sa_ref, sq_ref,
         xn_s, exta_s, extb_s, pre_s) = refs
        sta_ref = stq_ref = None
    ha = exta_s.shape[0] - tm
    hb = extb_s.shape[0] - tm
    na = (CONV_A_W - 1) * shift
    nb = (CONV_B_W - 1) * shift
    first = (pl.program_id(0) % tiles_per_seq) == 0

    x = x_ref[...]
    var = jnp.mean(x * x, axis=-1, keepdims=True)
    xn = x * lax.rsqrt(var + EPS) * nw_ref[...]
    xn_s[...] = xn.astype(BF16)

    nar = _dot_hi(xn, wn_ref[...])
    lane = lax.broadcasted_iota(jnp.int32, nar.shape, 1)
    a_row = hp_ref[0:1, :]
    dt_row = hp_ref[1:2, :]
    beta = jax.nn.sigmoid(nar)
    gdec = -jnp.exp(a_row) * _softplus(nar + dt_row)
    nar_ref[...] = jnp.where(lane < N_HEADS, beta, jnp.where(lane < 2 * N_HEADS, gdec, 0.0))

    def load_header(ext_s, st_ref, hrows, c0, c1):
        @pl.when(first)
        def _():
            if has_state:
                ext_s[0:hrows, c0:c1] = st_ref[:, c0:c1]
            else:
                ext_s[0:hrows, c0:c1] = jnp.zeros((hrows, c1 - c0), F32)

        @pl.when(jnp.logical_not(first))
        def _():
            ext_s[0:hrows, c0:c1] = ext_s[tm:tm + hrows, c0:c1]

    for c0 in range(0, D_MODEL, COL_CHUNK):
        c1 = c0 + COL_CHUNK
        xb = xn_s[...]
        pb = _dot(xb, wm_ref[:, OFF_B + c0:OFF_B + c1])
        pc = _dot(xb, wm_ref[:, OFF_C + c0:OFF_C + c1])
        ph = _dot(xb, wm_ref[:, OFF_H + c0:OFF_H + c1])
        pz = _dot(xb, wm_ref[:, OFF_Z + c0:OFF_Z + c1])
        ch = pc * ph
        load_header(exta_s, sta_ref, ha, c0, c1)
        exta_s[ha:ha + tm, c0:c1] = ch
        if has_state:
            sa_ref[:, c0:c1] = ch
        conv = ch * caw_ref[CONV_A_W - 1:CONV_A_W, c0:c1]
        for j in range(CONV_A_W - 1):
            d = (CONV_A_W - 1 - j) * shift
            conv = conv + exta_s[ha - d:ha - d + tm, c0:c1] * caw_ref[j:j + 1, c0:c1]
        pre_s[:, c0:c1] = (_silu(pz) * pb * conv).astype(BF16)
    if not has_state:
        sa_ref[...] = exta_s[ha + tm - na:ha + tm, :]

    for c0 in range(0, D_MODEL, COL_CHUNK):
        c1 = c0 + COL_CHUNK
        ya = _dot(pre_s[...], woa_ref[:, c0:c1])
        ga = _dot(xn_s[...], wm_ref[:, OFF_GA + c0:OFF_GA + c1])
        ma_ref[:, c0:c1] = (jax.nn.sigmoid(ga) * ya).astype(ma_ref.dtype)

    for g, out_ref in enumerate((q_ref, k_ref, v_ref)):
        for c0 in range(0, D_MODEL, COL_CHUNK):
            c1 = c0 + COL_CHUNK
            e0 = g * D_MODEL + c0
            e1 = e0 + COL_CHUNK
            p = _dot(xn_s[...], wm_ref[:, OFF_QKV + e0:OFF_QKV + e1])
            load_header(extb_s, stq_ref, hb, e0, e1)
            extb_s[hb:hb + tm, e0:e1] = p
            if has_state:
                sq_ref[:, e0:e1] = p
            conv = p * cbw_ref[CONV_B_W - 1:CONV_B_W, e0:e1]
            for j in range(CONV_B_W - 1):
                d = (CONV_B_W - 1 - j) * shift
                conv = conv + extb_s[hb - d:hb - d + tm, e0:e1] * cbw_ref[j:j + 1, e0:e1]
            s = _silu(conv)
            if g < 2:
                scale = HEAD_DIM ** -0.5 if g == 0 else 1.0
                for h0 in range(0, COL_CHUNK, HEAD_DIM):
                    sh = s[:, h0:h0 + HEAD_DIM]
                    ss = jnp.sum(sh * sh, axis=-1, keepdims=True)
                    out_ref[:, c0 + h0:c0 + h0 + HEAD_DIM] = (
                        sh * (lax.rsqrt(ss + EPS) * scale)).astype(out_ref.dtype)
            else:
                out_ref[:, c0:c1] = s.astype(out_ref.dtype)
    if not has_state:
        sq_ref[...] = extb_s[hb + tm - nb:hb + tm, :]

    for c0 in range(0, D_MODEL, COL_CHUNK):
        c1 = c0 + COL_CHUNK
        xb = xn_s[...]
        zb_ref[:, c0:c1] = _silu(_dot(xb, wm_ref[:, OFF_ZB + c0:OFF_ZB + c1])).astype(zb_ref.dtype)
        sgb_ref[:, c0:c1] = jax.nn.sigmoid(
            _dot(xb, wm_ref[:, OFF_GB + c0:OFF_GB + c1])).astype(sgb_ref.dtype)


def _resident(shape):
    nd = len(shape)
    return pl.BlockSpec(shape, lambda *_: (0,) * nd, pipeline_mode=pl.Buffered(1))


def _front(x2d, consts, *, tm, shift, n_seq, state=None):
    rows = x2d.shape[0]
    n_tiles = rows // tm
    tiles_per_seq = n_tiles // n_seq
    na = (CONV_A_W - 1) * shift
    nb = (CONV_B_W - 1) * shift
    hdr_a = -(-na // 8) * 8
    hdr_b = -(-nb // 8) * 8
    has_state = state is not None
    nw, wm, wn, caw, cbw, hp, woa = consts

    row_spec = lambda w: pl.BlockSpec((tm, w), lambda i: (i, 0))
    in_specs = [row_spec(D_MODEL), _resident(nw.shape), _resident(wm.shape), _resident(wn.shape),
                _resident(caw.shape), _resident(cbw.shape), _resident(hp.shape),
                _resident(woa.shape)]
    args = [x2d, nw, wm, wn, caw, cbw, hp, woa]
    out_specs = [row_spec(D_MODEL)] * 6 + [row_spec(NARROW)]
    out_shape = [jax.ShapeDtypeStruct((rows, D_MODEL), BF16)] * 6 + [
        jax.ShapeDtypeStruct((rows, NARROW), F32)]
    if has_state:
        sta, stq = state
        assert sta.shape[0] == hdr_a == na and stq.shape[0] == hdr_b == nb
        in_specs += [_resident(sta.shape), _resident(stq.shape)]
        args += [sta, stq]
        out_specs += [row_spec(D_MODEL), row_spec(QKV_W)]
        out_shape += [jax.ShapeDtypeStruct((rows, D_MODEL), F32),
                      jax.ShapeDtypeStruct((rows, QKV_W), F32)]
    else:
        seq_spec = lambda r, w: pl.BlockSpec((None, r, w), lambda i: (i // tiles_per_seq, 0, 0))
        out_specs += [seq_spec(na, D_MODEL), seq_spec(nb, QKV_W)]
        out_shape += [jax.ShapeDtypeStruct((n_seq, na, D_MODEL), F32),
                      jax.ShapeDtypeStruct((n_seq, nb, QKV_W), F32)]
    scratch = [pltpu.VMEM((tm, D_MODEL), BF16),
               pltpu.VMEM((hdr_a + tm, D_MODEL), F32),
               pltpu.VMEM((hdr_b + tm, QKV_W), F32),
               pltpu.VMEM((tm, D_MODEL), BF16)]
    return pl.pallas_call(
        functools.partial(_front_kernel, tm, shift, tiles_per_seq, has_state),
        grid=(n_tiles,),
        in_specs=in_specs,
        out_specs=out_specs,
        out_shape=out_shape,
        scratch_shapes=scratch,
        compiler_params=pltpu.CompilerParams(
            dimension_semantics=("arbitrary",), vmem_limit_bytes=V7X_VMEM_LIMIT),
        name="front",
    )(*args)


def _delta_kernel(chunk, cpb, has_init, *refs):
    if has_init:
        (q_ref, k_ref, v_ref, nar_ref, s0_ref, o_ref, so_ref,
         st_s, up_s, wdq_s, kd_s, aqk_s, vbk_s, dl_s) = refs
    else:
        (q_ref, k_ref, v_ref, nar_ref, o_ref, so_ref,
         st_s, up_s, wdq_s, kd_s, aqk_s, vbk_s, dl_s) = refs
    C = chunk
    levels = int(math.log2(C))
    grp = min(N_HEADS, 256 // C)
    W = grp * C
    sdt = wdq_s.dtype
    mx = lambda a: a.astype(BF16)

    @pl.when(pl.program_id(1) == 0)
    def _():
        if has_init:
            st_s[...] = s0_ref[...]
        else:
            st_s[...] = jnp.zeros(st_s.shape, F32)

    row = lax.broadcasted_iota(jnp.int32, (C, C), 0)
    col = lax.broadcasted_iota(jnp.int32, (C, C), 1)
    causal = row >= col
    strict = row > col
    ltri = causal.astype(F32)
    utri = (row <= col).astype(F32)
    rw = lax.broadcasted_iota(jnp.int32, (W, W), 0)
    cw = lax.broadcasted_iota(jnp.int32, (W, W), 1)
    blockdiag = (rw >> levels) == (cw >> levels)
    ri = lax.broadcasted_iota(jnp.int32, (C, W), 0)
    ci = lax.broadcasted_iota(jnp.int32, (C, W), 1)
    eye_cat = (ri == (ci & (C - 1))).astype(F32)

    for c in range(cpb):
        r0 = c * C
        narc = nar_ref[r0:r0 + C, :]
        gc = _dot_hi(ltri, narc)
        gct = lax.dot_general(narc, utri, (((0,), (0,)), ((), ())),
                              precision=HI, preferred_element_type=F32)
        glast = gc[C - 1:C, :]
        dl_s[c] = jnp.exp(glast)
        neg_m = []
        for h in range(N_HEADS):
            hs = slice(h * HEAD_DIM, (h + 1) * HEAD_DIM)
            qf = q_ref[r0:r0 + C, hs].astype(F32)
            kf = k_ref[r0:r0 + C, hs].astype(F32)
            vf = v_ref[r0:r0 + C, hs].astype(F32)
            gcc = gc[:, N_HEADS + h:N_HEADS + h + 1]
            gcr = gct[N_HEADS + h:N_HEADS + h + 1, :]
            bet = narc[:, h:h + 1]
            decay = jnp.exp(jnp.where(causal, gcc - gcr, -jnp.inf))
            egc = jnp.exp(gcc)
            kb = kf * bet
            vbk_s[h, :, 0:HEAD_DIM] = (vf * bet).astype(sdt)
            vbk_s[h, :, HEAD_DIM:2 * HEAD_DIM] = (kb * egc).astype(sdt)
            wdq_s[c, h, C:2 * C, :] = (qf * egc).astype(sdt)
            kd_s[c, h] = (kf * jnp.exp(glast[:, N_HEADS + h:N_HEADS + h + 1] - gcc)).astype(sdt)
            a = lax.dot_general(mx(jnp.concatenate([kb, qf], axis=0)), mx(kf),
                                (((1,), (1,)), ((), ())), preferred_element_type=F32)
            neg_m.append(-jnp.where(strict, a[:C] * decay, 0.0))
            aqk_s[c, h] = jnp.where(causal, a[C:] * decay, 0.0).astype(sdt)
        for g0 in range(0, N_HEADS, grp):
            p = jnp.concatenate(neg_m[g0:g0 + grp], axis=1)
            t = eye_cat + p
            for lv in range(levels):
                bd = jnp.where(blockdiag, jnp.concatenate([p] * grp, axis=0), 0.0)
                if lv == 0:
                    p = _dot_hi(p, bd)
                elif lv < levels - 1:
                    xx = _dot_hi(jnp.concatenate([t, p], axis=0), bd)
                    t = t + xx[:C]
                    p = xx[C:]
                else:
                    t = t + _dot_hi(t, bd)
            for j in range(grp):
                h = g0 + j
                uw = _dot(mx(t[:, j * C:(j + 1) * C]), mx(vbk_s[h]))
                up_s[c, h] = uw[:, 0:HEAD_DIM]
                wdq_s[c, h, 0:C, :] = uw[:, HEAD_DIM:2 * HEAD_DIM].astype(sdt)

    for c in range(cpb):
        r0 = c * C
        for h in range(N_HEADS):
            hs = slice(h * HEAD_DIM, (h + 1) * HEAD_DIM)
            s_old = st_s[h]
            r = _dot(mx(wdq_s[c, h]), mx(s_old))
            u = mx(up_s[c, h] - r[:C])
            o = r[C:] + _dot(mx(aqk_s[c, h]), u)
            o_ref[r0:r0 + C, hs] = o.astype(o_ref.dtype)
            dl = dl_s[c][:, N_HEADS + h:N_HEADS + h + 1]
            st_s[h] = s_old * dl + lax.dot_general(
                mx(kd_s[c, h]), u, (((0,), (0,)), ((), ())), preferred_element_type=F32)

    so_ref[...] = st_s[...]


def _delta(q, k, v, nar, *, n_seq, chunk, cpb, s0=None):
    rows = q.shape[0]
    seq_len = rows // n_seq
    tb = chunk * cpb
    n_blk = seq_len // tb
    has_init = s0 is not None
    row_spec = lambda w: pl.BlockSpec((tb, w), lambda b, i: (b * n_blk + i, 0))
    st_spec = pl.BlockSpec((None, N_HEADS, HEAD_DIM, HEAD_DIM), lambda b, i: (b, 0, 0, 0))
    in_specs = [row_spec(D_MODEL)] * 3 + [row_spec(NARROW)]
    args = [q, k, v, nar]
    if has_init:
        in_specs.append(st_spec)
        args.append(s0)
    sdt = BF16 if chunk % 16 == 0 else F32
    scratch = [pltpu.VMEM((N_HEADS, HEAD_DIM, HEAD_DIM), F32),
               pltpu.VMEM((cpb, N_HEADS, chunk, HEAD_DIM), F32),
               pltpu.VMEM((cpb, N_HEADS, 2 * chunk, HEAD_DIM), sdt),
               pltpu.VMEM((cpb, N_HEADS, chunk, HEAD_DIM), sdt),
               pltpu.VMEM((cpb, N_HEADS, chunk, chunk), sdt),
               pltpu.VMEM((N_HEADS, chunk, 2 * HEAD_DIM), sdt),
               pltpu.VMEM((cpb, 1, NARROW), F32)]
    return pl.pallas_call(
        functools.partial(_delta_kernel, chunk, cpb, has_init),
        grid=(n_seq, n_blk),
        in_specs=in_specs,
        out_specs=[row_spec(D_MODEL), st_spec],
        out_shape=[jax.ShapeDtypeStruct((rows, D_MODEL), q.dtype),
                   jax.ShapeDtypeStruct((n_seq, N_HEADS, HEAD_DIM, HEAD_DIM), F32)],
        scratch_shapes=scratch,
        compiler_params=pltpu.CompilerParams(
            dimension_semantics=("arbitrary", "arbitrary"), vmem_limit_bytes=V7X_VMEM_LIMIT),
        name="delta",
    )(*args)


def _back_kernel(x_ref, o_ref, zb_ref, ma_ref, sgb_ref, onw_ref, wob_ref, wo_ref, fw_ref,
                 y_ref, on_s, m_s):
    for h in range(N_HEADS):
        hs = slice(h * HEAD_DIM, (h + 1) * HEAD_DIM)
        oh = o_ref[:, hs].astype(F32)
        ms = jnp.mean(oh * oh, axis=-1, keepdims=True)
        on = oh * lax.rsqrt(ms + EPS) * onw_ref[...]
        on_s[:, hs] = (on * zb_ref[:, hs].astype(F32)).astype(BF16)
    for c0 in range(0, D_MODEL, COL_CHUNK):
        c1 = c0 + COL_CHUNK
        yb = _dot(on_s[...], wob_ref[:, c0:c1])
        m = ma_ref[:, c0:c1].astype(F32) + sgb_ref[:, c0:c1].astype(F32) * yb
        m_s[:, c0:c1] = m.astype(BF16)
    hres = x_ref[...] + _dot(m_s[...], wo_ref[...])
    var = jnp.mean(hres * hres, axis=-1, keepdims=True)
    y_ref[...] = hres * lax.rsqrt(var + EPS) * fw_ref[...]


def _back(x2d, o, zb, ma, sgb, consts, *, tm):
    rows = x2d.shape[0]
    onw, wob, wo, fw = consts
    row_spec = pl.BlockSpec((tm, D_MODEL), lambda i: (i, 0))
    return pl.pallas_call(
        _back_kernel,
        grid=(rows // tm,),
        in_specs=[row_spec] * 5 + [_resident(onw.shape), _resident(wob.shape),
                                   _resident(wo.shape), _resident(fw.shape)],
        out_specs=row_spec,
        out_shape=jax.ShapeDtypeStruct((rows, D_MODEL), F32),
        scratch_shapes=[pltpu.VMEM((tm, D_MODEL), BF16), pltpu.VMEM((tm, D_MODEL), BF16)],
        compiler_params=pltpu.CompilerParams(
            dimension_semantics=("arbitrary",), vmem_limit_bytes=V7X_VMEM_LIMIT),
        name="back",
    )(x2d, o, zb, ma, sgb, onw, wob, wo, fw)


def kernel(x_prompt, x_sample, state_conv_a, state_conv_qkv, state_delta, w_in, conv_a_w,
           conv_b_w, a_log, dt_bias, onorm_w, w_out_a, w_out_b, w_o, norm_w, final_norm_w):
    assert w_in.shape[0] == 1, "single layer"
    bp, tp, _ = x_prompt.shape
    bs, ts, _ = x_sample.shape
    w = w_in[0]
    n_wide = OFF_ZB + D_MODEL
    wm = jnp.concatenate([w[:, :n_wide], w[:, n_wide + 2 * N_HEADS:]], axis=1).astype(BF16)
    wn = jnp.pad(w[:, n_wide:n_wide + 2 * N_HEADS], ((0, 0), (0, NARROW - 2 * N_HEADS)))
    hp = jnp.zeros((8, NARROW), F32)
    hp = hp.at[0, N_HEADS:2 * N_HEADS].set(a_log[0]).at[1, N_HEADS:2 * N_HEADS].set(dt_bias[0])
    front_consts = (norm_w[0][None, :], wm, wn, conv_a_w[0], conv_b_w[0], hp,
                    w_out_a[0].astype(BF16))
    back_consts = (onorm_w[0][None, :], w_out_b[0].astype(BF16), w_o[0].astype(BF16),
                   final_norm_w[None, :])

    xp = x_prompt.reshape(bp * tp, D_MODEL)
    ma, q, k, v, zb, sgb, nar, sa, sq = _front(xp, front_consts, tm=256, shift=1, n_seq=bp)
    o, s_new = _delta(q, k, v, nar, n_seq=bp, chunk=PROMPT_CHUNK, cpb=4)
    y_prompt = _back(xp, o, zb, ma, sgb, back_consts, tm=256).reshape(bp, tp, D_MODEL)
    new_conv_a_prompt = sa[None]
    new_conv_qkv_prompt = sq[None]
    new_delta_prompt = s_new[None]

    xs = jnp.transpose(x_sample, (1, 0, 2)).reshape(ts * bs, D_MODEL)
    sta = jnp.transpose(state_conv_a[0], (1, 0, 2)).reshape((CONV_A_W - 1) * bs, D_MODEL)
    stq = jnp.transpose(state_conv_qkv[0], (1, 0, 2)).reshape((CONV_B_W - 1) * bs, QKV_W)
    ma, q, k, v, zb, sgb, nar, cha, pqkv = _front(
        xs, front_consts, tm=bs, shift=bs, n_seq=1, state=(sta, stq))
    sa = jnp.concatenate([sta, cha], axis=0)[ts * bs:]
    sq = jnp.concatenate([stq, pqkv], axis=0)[ts * bs:]
    tpad = 8

    def to_batch_major(a):
        a = jnp.transpose(a.reshape(ts, bs, a.shape[-1]), (1, 0, 2)).astype(F32)
        return jnp.pad(a, ((0, 0), (0, tpad - ts), (0, 0))).reshape(bs * tpad, a.shape[-1])

    o, s_new = _delta(to_batch_major(q), to_batch_major(k), to_batch_major(v),
                      to_batch_major(nar), n_seq=bs, chunk=tpad, cpb=1, s0=state_delta[0])
    o = jnp.transpose(o.reshape(bs, tpad, D_MODEL)[:, :ts], (1, 0, 2)).reshape(ts * bs, D_MODEL)
    ys = _back(xs, o, zb, ma, sgb, back_consts, tm=ts * bs)
    y_sample = jnp.transpose(ys.reshape(ts, bs, D_MODEL), (1, 0, 2))
    new_conv_a_sample = jnp.transpose(sa.reshape(CONV_A_W - 1, bs, D_MODEL), (1, 0, 2))[None]
    new_conv_qkv_sample = jnp.transpose(sq.reshape(CONV_B_W - 1, bs, QKV_W), (1, 0, 2))[None]
    new_delta_sample = s_new[None]

    return (y_prompt, y_sample, new_conv_a_prompt, new_conv_qkv_prompt, new_delta_prompt,
            new_conv_a_sample, new_conv_qkv_sample, new_delta_sample)
```

```python
import functools
import math

import jax
import jax.numpy as jnp
from jax import lax
from jax.experimental import pallas as pl
from jax.experimental.pallas import tpu as pltpu

D_MODEL = 1024
N_HEADS = 8
HEAD_DIM = 128
QKV_W = 3 * D_MODEL
CONV_A_W = 3
CONV_B_W = 4
PROMPT_CHUNK = 64
EPS = 1e-6
N_MAIN = 10 * D_MODEL
NARROW = 128
OFF_B, OFF_C, OFF_H, OFF_Z, OFF_QKV, OFF_ZB, OFF_GA, OFF_GB = (
    0, 1024, 2048, 3072, 4096, 7168, 8192, 9216)

V7X_VMEM_LIMIT = 56 * 1024 * 1024
COL_CHUNK = 256

F32 = jnp.float32
BF16 = jnp.bfloat16
HI = lax.Precision.HIGHEST


def _dot(a, b):
    return jnp.dot(a, b, preferred_element_type=F32)


def _dot_hi(a, b):
    return jnp.dot(a, b, preferred_element_type=F32, precision=HI)


def _silu(x):
    return x * jax.nn.sigmoid(x)


def _softplus(x):
    return jnp.maximum(x, 0.0) + jnp.log(1.0 + jnp.exp(-jnp.abs(x)))


def _front_kernel(tm, shift, tiles_per_seq, has_state, *refs):
    if has_state:
        (x_ref, nw_ref, wm_ref, wn_ref, caw_ref, cbw_ref, hp_ref, woa_ref, sta_ref, stq_ref,
         ma_ref, q_ref, k_ref, v_ref, zb_ref, sgb_ref, nar_ref, sa_ref, sq_ref,
         xn_s, exta_s, extb_s, pre_s) = refs
    else:
        (x_ref, nw_ref, wm_ref, wn_ref, caw_ref, cbw_ref, hp_ref, woa_ref,
         ma_ref, q_ref, k_ref, v_ref, zb_ref, sgb_ref, nar_ref, sa_ref, sq_ref,
         xn_s, exta_s, extb_s, pre_s) = refs
        sta_ref = stq_ref = None
    ha = exta_s.shape[0] - tm
    hb = extb_s.shape[0] - tm
    na = (CONV_A_W - 1) * shift
    nb = (CONV_B_W - 1) * shift
    first = (pl.program_id(0) % tiles_per_seq) == 0

    x = x_ref[...]
    var = jnp.mean(x * x, axis=-1, keepdims=True)
    xn = x * lax.rsqrt(var + EPS) * nw_ref[...]
    xn_s[...] = xn.astype(BF16)

    nar = _dot_hi(xn, wn_ref[...])
    lane = lax.broadcasted_iota(jnp.int32, nar.shape, 1)
    a_row = hp_ref[0:1, :]
    dt_row = hp_ref[1:2, :]
    beta = jax.nn.sigmoid(nar)
    gdec = -jnp.exp(a_row) * _softplus(nar + dt_row)
    nar_ref[...] = jnp.where(lane < N_HEADS, beta, jnp.where(lane < 2 * N_HEADS, gdec, 0.0))

    def load_header(ext_s, st_ref, hrows, c0, c1):
        @pl.when(first)
        def _():
            if has_state:
                ext_s[0:hrows, c0:c1] = st_ref[:, c0:c1]
            else:
                ext_s[0:hrows, c0:c1] = jnp.zeros((hrows, c1 - c0), F32)

        @pl.when(jnp.logical_not(first))
        def _():
            ext_s[0:hrows, c0:c1] = ext_s[tm:tm + hrows, c0:c1]

    for c0 in range(0, D_MODEL, COL_CHUNK):
        c1 = c0 + COL_CHUNK
        xb = xn_s[...]
        pb = _dot(xb, wm_ref[:, OFF_B + c0:OFF_B + c1])
        pc = _dot(xb, wm_ref[:, OFF_C + c0:OFF_C + c1])
        ph = _dot(xb, wm_ref[:, OFF_H + c0:OFF_H + c1])
        pz = _dot(xb, wm_ref[:, OFF_Z + c0:OFF_Z + c1])
        ch = pc * ph
        load_header(exta_s, sta_ref, ha, c0, c1)
        exta_s[ha:ha + tm, c0:c1] = ch
        if has_state:
            sa_ref[:, c0:c1] = ch
        conv = ch * caw_ref[CONV_A_W - 1:CONV_A_W, c0:c1]
        for j in range(CONV_A_W - 1):
            d = (CONV_A_W - 1 - j) * shift
            conv = conv + exta_s[ha - d:ha - d + tm, c0:c1] * caw_ref[j:j + 1, c0:c1]
        pre_s[:, c0:c1] = (_silu(pz) * pb * conv).astype(BF16)
    if not has_state:
        sa_ref[...] = exta_s[ha + tm - na:ha + tm, :]

    for c0 in range(0, D_MODEL, COL_CHUNK):
        c1 = c0 + COL_CHUNK
        ya = _dot(pre_s[...], woa_ref[:, c0:c1])
        ga = _dot(xn_s[...], wm_ref[:, OFF_GA + c0:OFF_GA + c1])
        ma_ref[:, c0:c1] = (jax.nn.sigmoid(ga) * ya).astype(ma_ref.dtype)

    for g, out_ref in enumerate((q_ref, k_ref, v_ref)):
        for c0 in range(0, D_MODEL, COL_CHUNK):
            c1 = c0 + COL_CHUNK
            e0 = g * D_MODEL + c0
            e1 = e0 + COL_CHUNK
            p = _dot(xn_s[...], wm_ref[:, OFF_QKV + e0:OFF_QKV + e1])
            load_header(extb_s, stq_ref, hb, e0, e1)
            extb_s[hb:hb + tm, e0:e1] = p
            if has_state:
                sq_ref[:, e0:e1] = p
            conv = p * cbw_ref[CONV_B_W - 1:CONV_B_W, e0:e1]
            for j in range(CONV_B_W - 1):
                d = (CONV_B_W - 1 - j) * shift
                conv = conv + extb_s[hb - d:hb - d + tm, e0:e1] * cbw_ref[j:j + 1, e0:e1]
            s = _silu(conv)
            if g < 2:
                scale = HEAD_DIM ** -0.5 if g == 0 else 1.0
                for h0 in range(0, COL_CHUNK, HEAD_DIM):
                    sh = s[:, h0:h0 + HEAD_DIM]
                    ss = jnp.sum(sh * sh, axis=-1, keepdims=True)
                    out_ref[:, c0 + h0:c0 + h0 + HEAD_DIM] = (
                        sh * (lax.rsqrt(ss + EPS) * scale)).astype(out_ref.dtype)
            else:
                out_ref[:, c0:c1] = s.astype(out_ref.dtype)
    if not has_state:
        sq_ref[...] = extb_s[hb + tm - nb:hb + tm, :]

    for c0 in range(0, D_MODEL, COL_CHUNK):
        c1 = c0 + COL_CHUNK
        xb = xn_s[...]
        zb_ref[:, c0:c1] = _silu(_dot(xb, wm_ref[:, OFF_ZB + c0:OFF_ZB + c1])).astype(zb_ref.dtype)
        sgb_ref[:, c0:c1] = jax.nn.sigmoid(
            _dot(xb, wm_ref[:, OFF_GB + c0:OFF_GB + c1])).astype(sgb_ref.dtype)


def _resident(shape):
    nd = len(shape)
    return pl.BlockSpec(shape, lambda *_: (0,) * nd, pipeline_mode=pl.Buffered(1))


def _front(x2d, consts, *, tm, shift, n_seq, state=None):
    rows = x2d.shape[0]
    n_tiles = rows // tm
    tiles_per_seq = n_tiles // n_seq
    na = (CONV_A_W - 1) * shift
    nb = (CONV_B_W - 1) * shift
    hdr_a = -(-na // 8) * 8
    hdr_b = -(-nb // 8) * 8
    has_state = state is not None
    nw, wm, wn, caw, cbw, hp, woa = consts

    row_spec = lambda w: pl.BlockSpec((tm, w), lambda i: (i, 0))
    in_specs = [row_spec(D_MODEL), _resident(nw.shape), _resident(wm.shape), _resident(wn.shape),
                _resident(caw.shape), _resident(cbw.shape), _resident(hp.shape),
                _resident(woa.shape)]
    args = [x2d, nw, wm, wn, caw, cbw, hp, woa]
    out_specs = [row_spec(D_MODEL)] * 6 + [row_spec(NARROW)]
    out_shape = [jax.ShapeDtypeStruct((rows, D_MODEL), BF16)] * 6 + [
        jax.ShapeDtypeStruct((rows, NARROW), F32)]
    if has_state:
        sta, stq = state
        assert sta.shape[0] == hdr_a == na and stq.shape[0] == hdr_b == nb
        in_specs += [_resident(sta.shape), _resident(stq.shape)]
        args += [sta, stq]
        out_specs += [row_spec(D_MODEL), row_spec(QKV_W)]
        out_shape += [jax.ShapeDtypeStruct((rows, D_MODEL), F32),
                      jax.ShapeDtypeStruct((rows, QKV_W), F32)]
    else:
        seq_spec = lambda r, w: pl.BlockSpec((None, r, w), lambda i: (i // tiles_per_seq, 0, 0))
        out_specs += [seq_spec(na, D_MODEL), seq_spec(nb, QKV_W)]
        out_shape += [jax.ShapeDtypeStruct((n_seq, na, D_MODEL), F32),
                      jax.ShapeDtypeStruct((n_seq, nb, QKV_W), F32)]
    scratch = [pltpu.VMEM((tm, D_MODEL), BF16),
               pltpu.VMEM((hdr_a + tm, D_MODEL), F32),
               pltpu.VMEM((hdr_b + tm, QKV_W), F32),
               pltpu.VMEM((tm, D_MODEL), BF16)]
    return pl.pallas_call(
        functools.partial(_front_kernel, tm, shift, tiles_per_seq, has_state),
        grid=(n_tiles,),
        in_specs=in_specs,
        out_specs=out_specs,
        out_shape=out_shape,
        scratch_shapes=scratch,
        compiler_params=pltpu.CompilerParams(
            dimension_semantics=("arbitrary",), vmem_limit_bytes=V7X_VMEM_LIMIT),
        name="front",
    )(*args)


def _delta_kernel(chunk, cpb, spb, has_init, *refs):
    if has_init:
        (q_ref, k_ref, v_ref, nar_ref, s0_ref, o_ref, so_ref,
         st_s, up_s, wdq_s, kdt_s, aqk_s, vbk_s, dl_s, tp_s) = refs
    else:
        (q_ref, k_ref, v_ref, nar_ref, o_ref, so_ref,
         st_s, up_s, wdq_s, kdt_s, aqk_s, vbk_s, dl_s, tp_s) = refs
    C = chunk
    nch = spb * cpb
    levels = int(math.log2(C))
    grp = min(N_HEADS, 256 // C)
    W = grp * C
    sdt = wdq_s.dtype
    mx = lambda a: a.astype(BF16)

    @pl.when(pl.program_id(1) == 0)
    def _():
        if has_init:
            st_s[...] = s0_ref[...]
        else:
            st_s[...] = jnp.zeros(st_s.shape, F32)

    row = lax.broadcasted_iota(jnp.int32, (C, C), 0)
    col = lax.broadcasted_iota(jnp.int32, (C, C), 1)
    causal = row >= col
    strict = row > col
    ltri = causal.astype(BF16)
    utri = (row <= col).astype(BF16)
    rw = lax.broadcasted_iota(jnp.int32, (W, W), 0)
    cw = lax.broadcasted_iota(jnp.int32, (W, W), 1)
    blockdiag = (rw >> levels) == (cw >> levels)
    ri = lax.broadcasted_iota(jnp.int32, (C, W), 0)
    ci = lax.broadcasted_iota(jnp.int32, (C, W), 1)
    eye_cat = (ri == (ci & (C - 1))).astype(F32)
    groups = [(c, g) for c in range(nch) for g in range(N_HEADS // grp)]

    def cumsum_t(a, tri, dims):
        hi = a.astype(BF16)
        lo = (a - hi.astype(F32)).astype(BF16)
        dn = (dims, ((), ()))
        if dims[0] == (1,):
            return (lax.dot_general(tri, hi, dn, preferred_element_type=F32)
                    + lax.dot_general(tri, lo, dn, preferred_element_type=F32))
        return (lax.dot_general(hi, tri, dn, preferred_element_type=F32)
                + lax.dot_general(lo, tri, dn, preferred_element_type=F32))

    heads = [(c, h) for c in range(nch) for h in range(N_HEADS)]
    hsl = lambda h: slice(h * HEAD_DIM, (h + 1) * HEAD_DIM)
    rsl = lambda c: slice(c * C, (c + 1) * C)
    gcs = [cumsum_t(nar_ref[rsl(c), :], ltri, ((1,), (0,))) for c in range(nch)]
    gcts = [cumsum_t(nar_ref[rsl(c), :], utri, ((0,), (0,))) for c in range(nch)]
    a_kq = {}
    for c, h in heads:
        kf = k_ref[rsl(c), hsl(h)].astype(F32)
        kb = kf * nar_ref[rsl(c), h:h + 1]
        qf = q_ref[rsl(c), hsl(h)].astype(F32)
        a_kq[c, h] = lax.dot_general(mx(jnp.concatenate([kb, qf], axis=0)), mx(kf),
                                     (((1,), (1,)), ((), ())), preferred_element_type=F32)
    for c in range(nch):
        dl_s[c] = jnp.exp(gcs[c][C - 1:C, :])
    for c, h in heads:
        g, j = divmod(h, grp)
        gc = gcs[c]
        glast = gc[C - 1:C, N_HEADS + h:N_HEADS + h + 1]
        gcc = gc[:, N_HEADS + h:N_HEADS + h + 1]
        gcr = gcts[c][N_HEADS + h:N_HEADS + h + 1, :]
        bet = nar_ref[rsl(c), h:h + 1]
        qf = q_ref[rsl(c), hsl(h)].astype(F32)
        kf = k_ref[rsl(c), hsl(h)].astype(F32)
        vf = v_ref[rsl(c), hsl(h)].astype(F32)
        decay = jnp.exp(jnp.where(causal, gcc - gcr, -jnp.inf))
        egc = jnp.exp(gcc)
        vbk_s[c, h, :, 0:HEAD_DIM] = (vf * bet).astype(sdt)
        vbk_s[c, h, :, HEAD_DIM:2 * HEAD_DIM] = (kf * bet * egc).astype(sdt)
        wdq_s[c, h, C:2 * C, :] = (qf * egc).astype(sdt)
        kdt_s[c, h] = (kf * jnp.exp(glast - gcc)).T.astype(sdt)
        a = a_kq[c, h]
        tp_s[c, g, C:2 * C, j * C:(j + 1) * C] = -jnp.where(strict, a[:C] * decay, 0.0)
        aqk_s[c, h] = jnp.where(causal, a[C:] * decay, 0.0).astype(sdt)

    for c, g in groups:
        tp_s[c, g, 0:C, :] = eye_cat + tp_s[c, g, C:2 * C, :]
    for lv in range(levels):
        for c, g in groups:
            p = tp_s[c, g, C:2 * C, :]
            bd = mx(jnp.where(blockdiag, jnp.concatenate([p] * grp, axis=0), 0.0))
            if lv == 0:
                tp_s[c, g, C:2 * C, :] = _dot(mx(p), bd)
            elif lv < levels - 1:
                xx = _dot(mx(tp_s[c, g]), bd)
                tp_s[c, g, 0:C, :] = tp_s[c, g, 0:C, :] + xx[:C]
                tp_s[c, g, C:2 * C, :] = xx[C:]
            else:
                t = tp_s[c, g, 0:C, :]
                tp_s[c, g, 0:C, :] = t + _dot(mx(t), bd)

    for c in range(nch):
        for h in range(N_HEADS):
            g, j = divmod(h, grp)
            uw = _dot(mx(tp_s[c, g, 0:C, j * C:(j + 1) * C]), mx(vbk_s[c, h]))
            up_s[c, h] = uw[:, 0:HEAD_DIM]
            wdq_s[c, h, 0:C, :] = uw[:, HEAD_DIM:2 * HEAD_DIM].astype(sdt)

    for cc in range(cpb):
        chains = [(s * cpb + cc, s, h) for s in range(spb) for h in range(N_HEADS)]
        rs = [_dot(mx(wdq_s[c, h]), mx(st_s[s, h])) for c, s, h in chains]
        us = [mx(up_s[c, h] - r[:C]) for (c, s, h), r in zip(chains, rs)]
        for (c, s, h), r, u in zip(chains, rs, us):
            o = r[C:] + _dot(mx(aqk_s[c, h]), u)
            o_ref[c * C:(c + 1) * C, h * HEAD_DIM:(h + 1) * HEAD_DIM] = o.astype(o_ref.dtype)
        for (c, s, h), u in zip(chains, us):
            dl = dl_s[c][:, N_HEADS + h:N_HEADS + h + 1]
            st_s[s, h] = st_s[s, h] * dl + _dot(mx(kdt_s[c, h]), u)

    so_ref[...] = st_s[...]


def _delta(q, k, v, nar, *, n_seq, chunk, cpb, spb=1, s0=None):
    rows = q.shape[0]
    seq_len = rows // n_seq
    tb = chunk * cpb
    n_blk = seq_len // tb
    assert spb == 1 or n_blk == 1
    has_init = s0 is not None
    nch = spb * cpb
    row_spec = lambda w: pl.BlockSpec((spb * tb, w), lambda b, i: (b * n_blk + i, 0))
    st_spec = pl.BlockSpec((spb, N_HEADS, HEAD_DIM, HEAD_DIM), lambda b, i: (b, 0, 0, 0))
    in_specs = [row_spec(D_MODEL)] * 3 + [row_spec(NARROW)]
    args = [q, k, v, nar]
    if has_init:
        in_specs.append(st_spec)
        args.append(s0)
    sdt = BF16 if chunk % 16 == 0 else F32
    grp = min(N_HEADS, 256 // chunk)
    scratch = [pltpu.VMEM((spb, N_HEADS, HEAD_DIM, HEAD_DIM), F32),
               pltpu.VMEM((nch, N_HEADS, chunk, HEAD_DIM), F32),
               pltpu.VMEM((nch, N_HEADS, 2 * chunk, HEAD_DIM), sdt),
               pltpu.VMEM((nch, N_HEADS, HEAD_DIM, chunk), sdt),
               pltpu.VMEM((nch, N_HEADS, chunk, chunk), sdt),
               pltpu.VMEM((nch, N_HEADS, chunk, 2 * HEAD_DIM), sdt),
               pltpu.VMEM((nch, 1, NARROW), F32),
               pltpu.VMEM((nch, N_HEADS // grp, 2 * chunk, grp * chunk), F32)]
    return pl.pallas_call(
        functools.partial(_delta_kernel, chunk, cpb, spb, has_init),
        grid=(n_seq // spb, n_blk),
        in_specs=in_specs,
        out_specs=[row_spec(D_MODEL), st_spec],
        out_shape=[jax.ShapeDtypeStruct((rows, D_MODEL), q.dtype),
                   jax.ShapeDtypeStruct((n_seq, N_HEADS, HEAD_DIM, HEAD_DIM), F32)],
        scratch_shapes=scratch,
        compiler_params=pltpu.CompilerParams(
            dimension_semantics=("arbitrary", "arbitrary"), vmem_limit_bytes=V7X_VMEM_LIMIT),
        name="delta",
    )(*args)


def _back_kernel(x_ref, o_ref, zb_ref, ma_ref, sgb_ref, onw_ref, wob_ref, wo_ref, fw_ref,
                 y_ref, on_s, m_s):
    for h in range(N_HEADS):
        hs = slice(h * HEAD_DIM, (h + 1) * HEAD_DIM)
        oh = o_ref[:, hs].astype(F32)
        ms = jnp.mean(oh * oh, axis=-1, keepdims=True)
        on = oh * lax.rsqrt(ms + EPS) * onw_ref[...]
        on_s[:, hs] = (on * zb_ref[:, hs].astype(F32)).astype(BF16)
    for c0 in range(0, D_MODEL, COL_CHUNK):
        c1 = c0 + COL_CHUNK
        yb = _dot(on_s[...], wob_ref[:, c0:c1])
        m = ma_ref[:, c0:c1].astype(F32) + sgb_ref[:, c0:c1].astype(F32) * yb
        m_s[:, c0:c1] = m.astype(BF16)
    hres = x_ref[...] + _dot(m_s[...], wo_ref[...])
    var = jnp.mean(hres * hres, axis=-1, keepdims=True)
    y_ref[...] = hres * lax.rsqrt(var + EPS) * fw_ref[...]


def _back(x2d, o, zb, ma, sgb, consts, *, tm):
    rows = x2d.shape[0]
    onw, wob, wo, fw = consts
    row_spec = pl.BlockSpec((tm, D_MODEL), lambda i: (i, 0))
    return pl.pallas_call(
        _back_kernel,
        grid=(rows // tm,),
        in_specs=[row_spec] * 5 + [_resident(onw.shape), _resident(wob.shape),
                                   _resident(wo.shape), _resident(fw.shape)],
        out_specs=row_spec,
        out_shape=jax.ShapeDtypeStruct((rows, D_MODEL), F32),
        scratch_shapes=[pltpu.VMEM((tm, D_MODEL), BF16), pltpu.VMEM((tm, D_MODEL), BF16)],
        compiler_params=pltpu.CompilerParams(
            dimension_semantics=("arbitrary",), vmem_limit_bytes=V7X_VMEM_LIMIT),
        name="back",
    )(x2d, o, zb, ma, sgb, onw, wob, wo, fw)


def kernel(x_prompt, x_sample, state_conv_a, state_conv_qkv, state_delta, w_in, conv_a_w,
           conv_b_w, a_log, dt_bias, onorm_w, w_out_a, w_out_b, w_o, norm_w, final_norm_w):
    assert w_in.shape[0] == 1, "single layer"
    bp, tp, _ = x_prompt.shape
    bs, ts, _ = x_sample.shape
    w = w_in[0]
    n_wide = OFF_ZB + D_MODEL
    wm = jnp.concatenate([w[:, :n_wide], w[:, n_wide + 2 * N_HEADS:]], axis=1).astype(BF16)
    wn = jnp.pad(w[:, n_wide:n_wide + 2 * N_HEADS], ((0, 0), (0, NARROW - 2 * N_HEADS)))
    hp = jnp.zeros((8, NARROW), F32)
    hp = hp.at[0, N_HEADS:2 * N_HEADS].set(a_log[0]).at[1, N_HEADS:2 * N_HEADS].set(dt_bias[0])
    front_consts = (norm_w[0][None, :], wm, wn, conv_a_w[0], conv_b_w[0], hp,
                    w_out_a[0].astype(BF16))
    back_consts = (onorm_w[0][None, :], w_out_b[0].astype(BF16), w_o[0].astype(BF16),
                   final_norm_w[None, :])

    xp = x_prompt.reshape(bp * tp, D_MODEL)
    ma, q, k, v, zb, sgb, nar, sa, sq = _front(xp, front_consts, tm=256, shift=1, n_seq=bp)
    o, s_new = _delta(q, k, v, nar, n_seq=bp, chunk=PROMPT_CHUNK, cpb=4)
    y_prompt = _back(xp, o, zb, ma, sgb, back_consts, tm=256).reshape(bp, tp, D_MODEL)
    new_conv_a_prompt = sa[None]
    new_conv_qkv_prompt = sq[None]
    new_delta_prompt = s_new[None]

    xs = jnp.transpose(x_sample, (1, 0, 2)).reshape(ts * bs, D_MODEL)
    sta = jnp.transpose(state_conv_a[0], (1, 0, 2)).reshape((CONV_A_W - 1) * bs, D_MODEL)
    stq = jnp.transpose(state_conv_qkv[0], (1, 0, 2)).reshape((CONV_B_W - 1) * bs, QKV_W)
    ma, q, k, v, zb, sgb, nar, cha, pqkv = _front(
        xs, front_consts, tm=bs, shift=bs, n_seq=1, state=(sta, stq))
    sa = jnp.concatenate([sta, cha], axis=0)[ts * bs:]
    sq = jnp.concatenate([stq, pqkv], axis=0)[ts * bs:]
    tpad = 8

    def to_batch_major(a):
        a = jnp.transpose(a.reshape(ts, bs, a.shape[-1]), (1, 0, 2)).astype(F32)
        return jnp.pad(a, ((0, 0), (0, tpad - ts), (0, 0))).reshape(bs * tpad, a.shape[-1])

    o, s_new = _delta(to_batch_major(q), to_batch_major(k), to_batch_major(v),
                      to_batch_major(nar), n_seq=bs, chunk=tpad, cpb=1, spb=8, s0=state_delta[0])
    o = jnp.transpose(o.reshape(bs, tpad, D_MODEL)[:, :ts], (1, 0, 2)).reshape(ts * bs, D_MODEL)
    ys = _back(xs, o, zb, ma, sgb, back_consts, tm=ts * bs)
    y_sample = jnp.transpose(ys.reshape(ts, bs, D_MODEL), (1, 0, 2))
    new_conv_a_sample = jnp.transpose(sa.reshape(CONV_A_W - 1, bs, D_MODEL), (1, 0, 2))[None]
    new_conv_qkv_sample = jnp.transpose(sq.reshape(CONV_B_W - 1, bs, QKV_W), (1, 0, 2))[None]
    new_delta_sample = s_new[None]

    return (y_prompt, y_sample, new_conv_a_prompt, new_conv_qkv_prompt, new_delta_prompt,
            new_conv_a_sample, new_conv_qkv_sample, new_delta_sample)
```

```python
import functools
import math

import jax
import jax.numpy as jnp
from jax import lax
from jax.experimental import pallas as pl
from jax.experimental.pallas import tpu as pltpu

D_MODEL = 1024
N_HEADS = 8
HEAD_DIM = 128
QKV_W = 3 * D_MODEL
CONV_A_W = 3
CONV_B_W = 4
PROMPT_CHUNK = 64
EPS = 1e-6
NARROW = 128
OFF_B, OFF_C, OFF_H, OFF_Z, OFF_QKV, OFF_ZB, OFF_GA, OFF_GB, OFF_NARROW = (
    0, 1024, 2048, 3072, 4096, 7168, 8192, 9216, 10240)

V7X_VMEM_LIMIT = 56 * 1024 * 1024
COL_CHUNK = 256

F32 = jnp.float32
BF16 = jnp.bfloat16


def _dot(a, b):
    return jnp.dot(a, b, preferred_element_type=F32)


def _silu(x):
    return x * jax.nn.sigmoid(x)


def _softplus(x):
    return jnp.maximum(x, 0.0) + jnp.log(1.0 + jnp.exp(-jnp.abs(x)))


def _front_kernel(tm, shift, tiles_per_seq, has_state, *refs):
    if has_state:
        (x_ref, nw_ref, wm_ref, caw_ref, cbw_ref, hp_ref, woa_ref, sta_ref, stq_ref,
         ma_ref, q_ref, k_ref, v_ref, zb_ref, sgb_ref, nar_ref, sa_ref, sq_ref,
         xn_s, exta_s, extb_s, pre_s, *stage_s) = refs
    else:
        (x_ref, nw_ref, wm_ref, caw_ref, cbw_ref, hp_ref, woa_ref,
         ma_ref, q_ref, k_ref, v_ref, zb_ref, sgb_ref, nar_ref, sa_ref, sq_ref,
         xn_s, exta_s, extb_s, pre_s, *stage_s) = refs
        sta_ref = stq_ref = None
    ha = exta_s.shape[0] - tm
    hb = extb_s.shape[0] - tm
    na = (CONV_A_W - 1) * shift
    nb = (CONV_B_W - 1) * shift
    first = (pl.program_id(0) % tiles_per_seq) == 0

    x = x_ref[...]
    var = jnp.mean(x * x, axis=-1, keepdims=True)
    xn_s[...] = (x * lax.rsqrt(var + EPS) * nw_ref[...]).astype(BF16)

    def proj(off, c0, width=COL_CHUNK):
        return _dot(xn_s[...], wm_ref[:, off + c0:off + c0 + width])

    @pl.when(pl.program_id(0) == 0)
    def _():
        exta_s[tm:tm + ha, :] = jnp.zeros((ha, D_MODEL), F32)
        extb_s[tm:tm + hb, :] = jnp.zeros((hb, QKV_W), F32)

    for ext_s, st_ref, hrows in ((exta_s, sta_ref, ha), (extb_s, stq_ref, hb)):
        for c0 in range(0, ext_s.shape[1], COL_CHUNK):
            cs = slice(c0, c0 + COL_CHUNK)
            init = st_ref[:, cs] if has_state else jnp.zeros((hrows, COL_CHUNK), F32)
            ext_s[0:hrows, cs] = jnp.where(first, init, ext_s[tm:tm + hrows, cs])

    def causal_conv(ext_s, hrows, width, w_ref, cur, c0, c1):
        ext_s[hrows:hrows + tm, c0:c1] = cur
        conv = cur * w_ref[width - 1:width, c0:c1]
        for j in range(width - 1):
            d = (width - 1 - j) * shift
            conv = conv + ext_s[hrows - d:hrows - d + tm, c0:c1] * w_ref[j:j + 1, c0:c1]
        return conv

    def narrow_dots():
        return (proj(OFF_NARROW, 0, NARROW),)

    def narrow_epi(nar):
        lane = lax.broadcasted_iota(jnp.int32, nar.shape, 1)
        beta = jax.nn.sigmoid(nar)
        gdec = -jnp.exp(hp_ref[0:1, :]) * _softplus(nar + hp_ref[1:2, :])
        nar_ref[...] = jnp.where(lane < N_HEADS, beta, jnp.where(lane < 2 * N_HEADS, gdec, 0.0))

    def branch_a_dots(c0):
        return tuple(proj(off, c0) for off in (OFF_B, OFF_C, OFF_H, OFF_Z))

    def branch_a_epi(c0, pb, pc, ph, pz):
        c1 = c0 + COL_CHUNK
        ch = pc * ph
        if has_state:
            sa_ref[:, c0:c1] = ch
        conv = causal_conv(exta_s, ha, CONV_A_W, caw_ref, ch, c0, c1)
        pre_s[:, c0:c1] = (_silu(pz) * pb * conv).astype(BF16)

    def gate_a_dots(c0):
        return (_dot(pre_s[...], woa_ref[:, c0:c0 + COL_CHUNK]), proj(OFF_GA, c0))

    def gate_a_epi(c0, ya, ga):
        ma_ref[:, c0:c0 + COL_CHUNK] = (jax.nn.sigmoid(ga) * ya).astype(ma_ref.dtype)

    def qkv_dots(e0):
        return (proj(OFF_QKV, e0),)

    def qkv_epi(e0, p):
        g, c0 = divmod(e0, D_MODEL)
        out_ref = (q_ref, k_ref, v_ref)[g]
        if has_state:
            sq_ref[:, e0:e0 + COL_CHUNK] = p
        s = _silu(causal_conv(extb_s, hb, CONV_B_W, cbw_ref, p, e0, e0 + COL_CHUNK))
        if g == 2:
            out_ref[:, c0:c0 + COL_CHUNK] = s.astype(out_ref.dtype)
            return
        scale = HEAD_DIM ** -0.5 if g == 0 else 1.0
        for h0 in range(0, COL_CHUNK, HEAD_DIM):
            sh = s[:, h0:h0 + HEAD_DIM]
            ss = jnp.sum(sh * sh, axis=-1, keepdims=True)
            out_ref[:, c0 + h0:c0 + h0 + HEAD_DIM] = (
                sh * (lax.rsqrt(ss + EPS) * scale)).astype(out_ref.dtype)

    def act_dots(off, c0):
        return (proj(off, c0),)

    def zb_epi(c0, p):
        zb_ref[:, c0:c0 + COL_CHUNK] = _silu(p).astype(zb_ref.dtype)

    def sgb_epi(c0, p):
        sgb_ref[:, c0:c0 + COL_CHUNK] = jax.nn.sigmoid(p).astype(sgb_ref.dtype)

    part = functools.partial
    cols = list(range(0, D_MODEL, COL_CHUNK))
    n_col = len(cols)
    stages = [[(part(qkv_dots, e0), part(qkv_epi, e0))] for e0 in range(0, QKV_W, COL_CHUNK)]
    for n, c0 in enumerate(cols):
        stages[2 * n].append((part(branch_a_dots, c0), part(branch_a_epi, c0)))
        stages[2 * n + 1] += [(part(act_dots, OFF_ZB, c0), part(zb_epi, c0)),
                              (part(act_dots, OFF_GB, c0), part(sgb_epi, c0))]
        stages[2 * n_col + n].append((part(gate_a_dots, c0), part(gate_a_epi, c0)))
    stages[2 * n_col].append((narrow_dots, narrow_epi))
    assert len(stages) == 3 * n_col

    def run_dots(i):
        widths, k = [], 0
        for dots, _ in stages[i]:
            outs = dots()
            for r in outs:
                stage_s[i % 2][k, :, 0:r.shape[1]] = r
                k += 1
            widths.append([r.shape[1] for r in outs])
        return widths

    widths = run_dots(0)
    for i, stage in enumerate(stages):
        ready = widths
        if i + 1 < len(stages):
            widths = run_dots(i + 1)
        k = 0
        for (_, epilogue), ws in zip(stage, ready):
            epilogue(*[stage_s[i % 2][k + n, :, 0:w] for n, w in enumerate(ws)])
            k += len(ws)
    if not has_state:
        sa_ref[...] = exta_s[ha + tm - na:ha + tm, :]
        sq_ref[...] = extb_s[hb + tm - nb:hb + tm, :]


def _resident(shape):
    nd = len(shape)
    return pl.BlockSpec(shape, lambda *_: (0,) * nd, pipeline_mode=pl.Buffered(1))


def _front(x2d, consts, *, tm, shift, n_seq, state=None):
    rows = x2d.shape[0]
    n_tiles = rows // tm
    tiles_per_seq = n_tiles // n_seq
    na = (CONV_A_W - 1) * shift
    nb = (CONV_B_W - 1) * shift
    hdr_a = -(-na // 8) * 8
    hdr_b = -(-nb // 8) * 8
    has_state = state is not None
    nw, wm, caw, cbw, hp, woa = consts

    row_spec = lambda w: pl.BlockSpec((tm, w), lambda i: (i, 0))
    in_specs = [row_spec(D_MODEL), _resident(nw.shape), _resident(wm.shape),
                _resident(caw.shape), _resident(cbw.shape), _resident(hp.shape),
                _resident(woa.shape)]
    args = [x2d, nw, wm, caw, cbw, hp, woa]
    out_specs = [row_spec(D_MODEL)] * 6 + [row_spec(NARROW)]
    out_shape = [jax.ShapeDtypeStruct((rows, D_MODEL), BF16)] * 6 + [
        jax.ShapeDtypeStruct((rows, NARROW), F32)]
    if has_state:
        sta, stq = state
        assert sta.shape[0] == hdr_a == na and stq.shape[0] == hdr_b == nb
        in_specs += [_resident(sta.shape), _resident(stq.shape)]
        args += [sta, stq]
        out_specs += [row_spec(D_MODEL), row_spec(QKV_W)]
        out_shape += [jax.ShapeDtypeStruct((rows, D_MODEL), F32),
                      jax.ShapeDtypeStruct((rows, QKV_W), F32)]
    else:
        seq_spec = lambda r, w: pl.BlockSpec((None, r, w), lambda i: (i // tiles_per_seq, 0, 0))
        out_specs += [seq_spec(na, D_MODEL), seq_spec(nb, QKV_W)]
        out_shape += [jax.ShapeDtypeStruct((n_seq, na, D_MODEL), F32),
                      jax.ShapeDtypeStruct((n_seq, nb, QKV_W), F32)]
    scratch = [pltpu.VMEM((tm, D_MODEL), BF16),
               pltpu.VMEM((hdr_a + tm, D_MODEL), F32),
               pltpu.VMEM((hdr_b + tm, QKV_W), F32),
               pltpu.VMEM((tm, D_MODEL), BF16),
               pltpu.VMEM((5, tm, COL_CHUNK), F32), pltpu.VMEM((5, tm, COL_CHUNK), F32)]
    return pl.pallas_call(
        functools.partial(_front_kernel, tm, shift, tiles_per_seq, has_state),
        grid=(n_tiles,),
        in_specs=in_specs,
        out_specs=out_specs,
        out_shape=out_shape,
        scratch_shapes=scratch,
        compiler_params=pltpu.CompilerParams(
            dimension_semantics=("arbitrary",), vmem_limit_bytes=V7X_VMEM_LIMIT),
        name="front",
    )(*args)


def _delta_kernel(chunk, cpb, spb, has_init, *refs):
    if has_init:
        (q_ref, k_ref, v_ref, nar_ref, s0_ref, o_ref, so_ref,
         st_s, up_s, wdq_s, kdt_s, aqk_s, vbk_s, dl_s, tp_s) = refs
    else:
        (q_ref, k_ref, v_ref, nar_ref, o_ref, so_ref,
         st_s, up_s, wdq_s, kdt_s, aqk_s, vbk_s, dl_s, tp_s) = refs
    C = chunk
    nch = spb * cpb
    levels = int(math.log2(C))
    grp = min(N_HEADS, 256 // C)
    W = grp * C
    sdt = wdq_s.dtype
    mx = lambda a: a.astype(BF16)

    @pl.when(pl.program_id(1) == 0)
    def _():
        if has_init:
            st_s[...] = s0_ref[...]
        else:
            st_s[...] = jnp.zeros(st_s.shape, F32)

    row = lax.broadcasted_iota(jnp.int32, (C, C), 0)
    col = lax.broadcasted_iota(jnp.int32, (C, C), 1)
    causal = row >= col
    strict = row > col
    ltri = causal.astype(BF16)
    utri = (row <= col).astype(BF16)
    rw = lax.broadcasted_iota(jnp.int32, (W, W), 0)
    cw = lax.broadcasted_iota(jnp.int32, (W, W), 1)
    blockdiag = (rw >> levels) == (cw >> levels)
    ri = lax.broadcasted_iota(jnp.int32, (C, W), 0)
    ci = lax.broadcasted_iota(jnp.int32, (C, W), 1)
    eye_cat = (ri == (ci & (C - 1))).astype(F32)
    groups = [(c, g) for c in range(nch) for g in range(N_HEADS // grp)]

    def cumsum_t(a, tri, dims):
        hi = a.astype(BF16)
        lo = (a - hi.astype(F32)).astype(BF16)
        dn = (dims, ((), ()))
        if dims[0] == (1,):
            return (lax.dot_general(tri, hi, dn, preferred_element_type=F32)
                    + lax.dot_general(tri, lo, dn, preferred_element_type=F32))
        return (lax.dot_general(hi, tri, dn, preferred_element_type=F32)
                + lax.dot_general(lo, tri, dn, preferred_element_type=F32))

    heads = [(c, h) for c in range(nch) for h in range(N_HEADS)]
    hsl = lambda h: slice(h * HEAD_DIM, (h + 1) * HEAD_DIM)
    rsl = lambda c: slice(c * C, (c + 1) * C)
    gcs = [cumsum_t(nar_ref[rsl(c), :], ltri, ((1,), (0,))) for c in range(nch)]
    gcts = [cumsum_t(nar_ref[rsl(c), :], utri, ((0,), (0,))) for c in range(nch)]
    a_kq = {}
    for c, h in heads:
        kf = k_ref[rsl(c), hsl(h)].astype(F32)
        kb = kf * nar_ref[rsl(c), h:h + 1]
        qf = q_ref[rsl(c), hsl(h)].astype(F32)
        a_kq[c, h] = lax.dot_general(mx(jnp.concatenate([kb, qf], axis=0)), mx(kf),
                                     (((1,), (1,)), ((), ())), preferred_element_type=F32)
    for c in range(nch):
        dl_s[c] = jnp.exp(gcs[c][C - 1:C, :])
    for c, h in heads:
        g, j = divmod(h, grp)
        gc = gcs[c]
        glast = gc[C - 1:C, N_HEADS + h:N_HEADS + h + 1]
        gcc = gc[:, N_HEADS + h:N_HEADS + h + 1]
        gcr = gcts[c][N_HEADS + h:N_HEADS + h + 1, :]
        bet = nar_ref[rsl(c), h:h + 1]
        qf = q_ref[rsl(c), hsl(h)].astype(F32)
        kf = k_ref[rsl(c), hsl(h)].astype(F32)
        vf = v_ref[rsl(c), hsl(h)].astype(F32)
        decay = jnp.exp(jnp.where(causal, gcc - gcr, -jnp.inf))
        egc = jnp.exp(gcc)
        vbk_s[c, h, :, 0:HEAD_DIM] = (vf * bet).astype(sdt)
        vbk_s[c, h, :, HEAD_DIM:2 * HEAD_DIM] = (kf * bet * egc).astype(sdt)
        wdq_s[c, h, C:2 * C, :] = (qf * egc).astype(sdt)
        kdt_s[c, h] = (kf * jnp.exp(glast - gcc)).T.astype(sdt)
        a = a_kq[c, h]
        tp_s[c, g, C:2 * C, j * C:(j + 1) * C] = -jnp.where(strict, a[:C] * decay, 0.0)
        aqk_s[c, h] = jnp.where(causal, a[C:] * decay, 0.0).astype(sdt)

    for c, g in groups:
        tp_s[c, g, 0:C, :] = eye_cat + tp_s[c, g, C:2 * C, :]
    for lv in range(levels):
        for c, g in groups:
            p = tp_s[c, g, C:2 * C, :]
            bd = mx(jnp.where(blockdiag, jnp.concatenate([p] * grp, axis=0), 0.0))
            if lv == 0:
                tp_s[c, g, C:2 * C, :] = _dot(mx(p), bd)
            elif lv < levels - 1:
                xx = _dot(mx(tp_s[c, g]), bd)
                tp_s[c, g, 0:C, :] = tp_s[c, g, 0:C, :] + xx[:C]
                tp_s[c, g, C:2 * C, :] = xx[C:]
            else:
                t = tp_s[c, g, 0:C, :]
                tp_s[c, g, 0:C, :] = t + _dot(mx(t), bd)

    for c in range(nch):
        for h in range(N_HEADS):
            g, j = divmod(h, grp)
            uw = _dot(mx(tp_s[c, g, 0:C, j * C:(j + 1) * C]), mx(vbk_s[c, h]))
            up_s[c, h] = uw[:, 0:HEAD_DIM]
            wdq_s[c, h, 0:C, :] = uw[:, HEAD_DIM:2 * HEAD_DIM].astype(sdt)

    for cc in range(cpb):
        chains = [(s * cpb + cc, s, h) for s in range(spb) for h in range(N_HEADS)]
        rs = [_dot(mx(wdq_s[c, h]), mx(st_s[s, h])) for c, s, h in chains]
        us = [mx(up_s[c, h] - r[:C]) for (c, s, h), r in zip(chains, rs)]
        for (c, s, h), r, u in zip(chains, rs, us):
            o = r[C:] + _dot(mx(aqk_s[c, h]), u)
            o_ref[c * C:(c + 1) * C, h * HEAD_DIM:(h + 1) * HEAD_DIM] = o.astype(o_ref.dtype)
        for (c, s, h), u in zip(chains, us):
            dl = dl_s[c][:, N_HEADS + h:N_HEADS + h + 1]
            st_s[s, h] = st_s[s, h] * dl + _dot(mx(kdt_s[c, h]), u)

    so_ref[...] = st_s[...]


def _delta(q, k, v, nar, *, n_seq, chunk, cpb, spb=1, s0=None):
    rows = q.shape[0]
    seq_len = rows // n_seq
    tb = chunk * cpb
    n_blk = seq_len // tb
    assert spb == 1 or n_blk == 1
    has_init = s0 is not None
    nch = spb * cpb
    row_spec = lambda w: pl.BlockSpec((spb * tb, w), lambda b, i: (b * n_blk + i, 0))
    st_spec = pl.BlockSpec((spb, N_HEADS, HEAD_DIM, HEAD_DIM), lambda b, i: (b, 0, 0, 0))
    in_specs = [row_spec(D_MODEL)] * 3 + [row_spec(NARROW)]
    args = [q, k, v, nar]
    if has_init:
        in_specs.append(st_spec)
        args.append(s0)
    sdt = BF16 if chunk % 16 == 0 else F32
    grp = min(N_HEADS, 256 // chunk)
    scratch = [pltpu.VMEM((spb, N_HEADS, HEAD_DIM, HEAD_DIM), F32),
               pltpu.VMEM((nch, N_HEADS, chunk, HEAD_DIM), F32),
               pltpu.VMEM((nch, N_HEADS, 2 * chunk, HEAD_DIM), sdt),
               pltpu.VMEM((nch, N_HEADS, HEAD_DIM, chunk), sdt),
               pltpu.VMEM((nch, N_HEADS, chunk, chunk), sdt),
               pltpu.VMEM((nch, N_HEADS, chunk, 2 * HEAD_DIM), sdt),
               pltpu.VMEM((nch, 1, NARROW), F32),
               pltpu.VMEM((nch, N_HEADS // grp, 2 * chunk, grp * chunk), F32)]
    return pl.pallas_call(
        functools.partial(_delta_kernel, chunk, cpb, spb, has_init),
        grid=(n_seq // spb, n_blk),
        in_specs=in_specs,
        out_specs=[row_spec(D_MODEL), st_spec],
        out_shape=[jax.ShapeDtypeStruct((rows, D_MODEL), q.dtype),
                   jax.ShapeDtypeStruct((n_seq, N_HEADS, HEAD_DIM, HEAD_DIM), F32)],
        scratch_shapes=scratch,
        compiler_params=pltpu.CompilerParams(
            dimension_semantics=("arbitrary", "arbitrary"), vmem_limit_bytes=V7X_VMEM_LIMIT),
        name="delta",
    )(*args)


def _back_kernel(x_ref, o_ref, zb_ref, ma_ref, sgb_ref, onw_ref, wob_ref, wo_ref, fw_ref,
                 y_ref, on_s, m_s):
    for h in range(N_HEADS):
        hs = slice(h * HEAD_DIM, (h + 1) * HEAD_DIM)
        oh = o_ref[:, hs].astype(F32)
        ms = jnp.mean(oh * oh, axis=-1, keepdims=True)
        on = oh * lax.rsqrt(ms + EPS) * onw_ref[...]
        on_s[:, hs] = (on * zb_ref[:, hs].astype(F32)).astype(BF16)
    for c0 in range(0, D_MODEL, COL_CHUNK):
        c1 = c0 + COL_CHUNK
        yb = _dot(on_s[...], wob_ref[:, c0:c1])
        m = ma_ref[:, c0:c1].astype(F32) + sgb_ref[:, c0:c1].astype(F32) * yb
        m_s[:, c0:c1] = m.astype(BF16)
    hres = x_ref[...] + _dot(m_s[...], wo_ref[...])
    var = jnp.mean(hres * hres, axis=-1, keepdims=True)
    y_ref[...] = hres * lax.rsqrt(var + EPS) * fw_ref[...]


def _back(x2d, o, zb, ma, sgb, consts, *, tm):
    rows = x2d.shape[0]
    onw, wob, wo, fw = consts
    row_spec = pl.BlockSpec((tm, D_MODEL), lambda i: (i, 0))
    return pl.pallas_call(
        _back_kernel,
        grid=(rows // tm,),
        in_specs=[row_spec] * 5 + [_resident(onw.shape), _resident(wob.shape),
                                   _resident(wo.shape), _resident(fw.shape)],
        out_specs=row_spec,
        out_shape=jax.ShapeDtypeStruct((rows, D_MODEL), F32),
        scratch_shapes=[pltpu.VMEM((tm, D_MODEL), BF16), pltpu.VMEM((tm, D_MODEL), BF16)],
        compiler_params=pltpu.CompilerParams(
            dimension_semantics=("arbitrary",), vmem_limit_bytes=V7X_VMEM_LIMIT),
        name="back",
    )(x2d, o, zb, ma, sgb, onw, wob, wo, fw)


def kernel(x_prompt, x_sample, state_conv_a, state_conv_qkv, state_delta, w_in, conv_a_w,
           conv_b_w, a_log, dt_bias, onorm_w, w_out_a, w_out_b, w_o, norm_w, final_norm_w):
    assert w_in.shape[0] == 1, "single layer"
    bp, tp, _ = x_prompt.shape
    bs, ts, _ = x_sample.shape
    w = w_in[0]
    n_wide = OFF_ZB + D_MODEL
    wn = jnp.pad(w[:, n_wide:n_wide + 2 * N_HEADS], ((0, 0), (0, NARROW - 2 * N_HEADS)))
    wm = jnp.concatenate([w[:, :n_wide], w[:, n_wide + 2 * N_HEADS:], wn], axis=1).astype(BF16)
    hp = jnp.zeros((8, NARROW), F32)
    hp = hp.at[0, N_HEADS:2 * N_HEADS].set(a_log[0]).at[1, N_HEADS:2 * N_HEADS].set(dt_bias[0])
    front_consts = (norm_w[0][None, :], wm, conv_a_w[0], conv_b_w[0], hp,
                    w_out_a[0].astype(BF16))
    back_consts = (onorm_w[0][None, :], w_out_b[0].astype(BF16), w_o[0].astype(BF16),
                   final_norm_w[None, :])

    xp = x_prompt.reshape(bp * tp, D_MODEL)
    ma, q, k, v, zb, sgb, nar, sa, sq = _front(xp, front_consts, tm=256, shift=1, n_seq=bp)
    o, s_new = _delta(q, k, v, nar, n_seq=bp, chunk=PROMPT_CHUNK, cpb=4)
    y_prompt = _back(xp, o, zb, ma, sgb, back_consts, tm=256).reshape(bp, tp, D_MODEL)
    new_conv_a_prompt = sa[None]
    new_conv_qkv_prompt = sq[None]
    new_delta_prompt = s_new[None]

    xs = jnp.transpose(x_sample, (1, 0, 2)).reshape(ts * bs, D_MODEL)
    sta = jnp.transpose(state_conv_a[0], (1, 0, 2)).reshape((CONV_A_W - 1) * bs, D_MODEL)
    stq = jnp.transpose(state_conv_qkv[0], (1, 0, 2)).reshape((CONV_B_W - 1) * bs, QKV_W)
    ma, q, k, v, zb, sgb, nar, cha, pqkv = _front(
        xs, front_consts, tm=bs, shift=bs, n_seq=1, state=(sta, stq))
    sa = jnp.concatenate([sta, cha], axis=0)[ts * bs:]
    sq = jnp.concatenate([stq, pqkv], axis=0)[ts * bs:]
    tpad = 8

    def to_batch_major(a):
        a = jnp.transpose(a.reshape(ts, bs, a.shape[-1]), (1, 0, 2)).astype(F32)
        return jnp.pad(a, ((0, 0), (0, tpad - ts), (0, 0))).reshape(bs * tpad, a.shape[-1])

    o, s_new = _delta(to_batch_major(q), to_batch_major(k), to_batch_major(v),
                      to_batch_major(nar), n_seq=bs, chunk=tpad, cpb=1, spb=8, s0=state_delta[0])
    o = jnp.transpose(o.reshape(bs, tpad, D_MODEL)[:, :ts], (1, 0, 2)).reshape(ts * bs, D_MODEL)
    ys = _back(xs, o, zb, ma, sgb, back_consts, tm=ts * bs)
    y_sample = jnp.transpose(ys.reshape(ts, bs, D_MODEL), (1, 0, 2))
    new_conv_a_sample = jnp.transpose(sa.reshape(CONV_A_W - 1, bs, D_MODEL), (1, 0, 2))[None]
    new_conv_qkv_sample = jnp.transpose(sq.reshape(CONV_B_W - 1, bs, QKV_W), (1, 0, 2))[None]
    new_delta_sample = s_new[None]

    return (y_prompt, y_sample, new_conv_a_prompt, new_conv_qkv_prompt, new_delta_prompt,
            new_conv_a_sample, new_conv_qkv_sample, new_delta_sample)
```

```python
import functools
import math

import jax
import jax.numpy as jnp
from jax import lax
from jax.experimental import pallas as pl
from jax.experimental.pallas import tpu as pltpu

D_MODEL = 1024
N_HEADS = 8
HEAD_DIM = 128
QKV_W = 3 * D_MODEL
CONV_A_W = 3
CONV_B_W = 4
PROMPT_CHUNK = 64
EPS = 1e-6
NARROW = 128
OFF_B, OFF_C, OFF_H, OFF_Z, OFF_QKV, OFF_ZB, OFF_GA, OFF_GB, OFF_NARROW = (
    0, 1024, 2048, 3072, 4096, 7168, 8192, 9216, 10240)

V7X_VMEM_LIMIT = 56 * 1024 * 1024
COL_CHUNK = 256

F32 = jnp.float32
BF16 = jnp.bfloat16


def _dot(a, b):
    return jnp.dot(a, b, preferred_element_type=F32)


def _silu(x):
    return x * jax.nn.sigmoid(x)


def _softplus(x):
    return jnp.maximum(x, 0.0) + jnp.log(1.0 + jnp.exp(-jnp.abs(x)))


def _front_kernel(tm, shift, tiles_per_seq, has_state, *refs):
    if has_state:
        (x_ref, nw_ref, wm_ref, caw_ref, cbw_ref, hp_ref, woa_ref, sta_ref, stq_ref,
         ma_ref, q_ref, k_ref, v_ref, zb_ref, sgb_ref, nar_ref, sa_ref, sq_ref,
         xn_s, exta_s, extb_s, pre_s, *stage_s) = refs
    else:
        (x_ref, nw_ref, wm_ref, caw_ref, cbw_ref, hp_ref, woa_ref,
         ma_ref, q_ref, k_ref, v_ref, zb_ref, sgb_ref, nar_ref, sa_ref, sq_ref,
         xn_s, exta_s, extb_s, pre_s, *stage_s) = refs
        sta_ref = stq_ref = None
    ha = exta_s.shape[0] - tm
    hb = extb_s.shape[0] - tm
    na = (CONV_A_W - 1) * shift
    nb = (CONV_B_W - 1) * shift
    first = (pl.program_id(0) % tiles_per_seq) == 0

    x = x_ref[...]
    var = jnp.mean(x * x, axis=-1, keepdims=True)
    xn_s[...] = (x * lax.rsqrt(var + EPS) * nw_ref[...]).astype(BF16)

    def proj(off, c0, width=COL_CHUNK):
        return _dot(xn_s[...], wm_ref[:, off + c0:off + c0 + width])

    @pl.when(pl.program_id(0) == 0)
    def _():
        exta_s[tm:tm + ha, :] = jnp.zeros((ha, D_MODEL), F32)
        extb_s[tm:tm + hb, :] = jnp.zeros((hb, QKV_W), F32)

    for ext_s, st_ref, hrows in ((exta_s, sta_ref, ha), (extb_s, stq_ref, hb)):
        for c0 in range(0, ext_s.shape[1], COL_CHUNK):
            cs = slice(c0, c0 + COL_CHUNK)
            init = st_ref[:, cs] if has_state else jnp.zeros((hrows, COL_CHUNK), F32)
            ext_s[0:hrows, cs] = jnp.where(first, init, ext_s[tm:tm + hrows, cs])

    def causal_conv(ext_s, hrows, width, w_ref, cur, c0, c1):
        ext_s[hrows:hrows + tm, c0:c1] = cur
        conv = cur * w_ref[width - 1:width, c0:c1]
        for j in range(width - 1):
            d = (width - 1 - j) * shift
            conv = conv + ext_s[hrows - d:hrows - d + tm, c0:c1] * w_ref[j:j + 1, c0:c1]
        return conv

    def narrow_dots():
        return (proj(OFF_NARROW, 0, NARROW),)

    def narrow_epi(nar):
        lane = lax.broadcasted_iota(jnp.int32, nar.shape, 1)
        beta = jax.nn.sigmoid(nar)
        gdec = -jnp.exp(hp_ref[0:1, :]) * _softplus(nar + hp_ref[1:2, :])
        nar_ref[...] = jnp.where(lane < N_HEADS, beta, jnp.where(lane < 2 * N_HEADS, gdec, 0.0))

    def branch_a_dots(c0):
        return tuple(proj(off, c0) for off in (OFF_B, OFF_C, OFF_H, OFF_Z))

    def branch_a_epi(c0, pb, pc, ph, pz):
        c1 = c0 + COL_CHUNK
        ch = pc * ph
        if has_state:
            sa_ref[:, c0:c1] = ch
        conv = causal_conv(exta_s, ha, CONV_A_W, caw_ref, ch, c0, c1)
        pre_s[:, c0:c1] = (_silu(pz) * pb * conv).astype(BF16)

    def gate_a_dots(c0):
        return (_dot(pre_s[...], woa_ref[:, c0:c0 + COL_CHUNK]), proj(OFF_GA, c0))

    def gate_a_epi(c0, ya, ga):
        ma_ref[:, c0:c0 + COL_CHUNK] = (jax.nn.sigmoid(ga) * ya).astype(ma_ref.dtype)

    def qkv_dots(e0):
        return (proj(OFF_QKV, e0),)

    def qkv_epi(e0, p):
        g, c0 = divmod(e0, D_MODEL)
        out_ref = (q_ref, k_ref, v_ref)[g]
        if has_state:
            sq_ref[:, e0:e0 + COL_CHUNK] = p
        s = _silu(causal_conv(extb_s, hb, CONV_B_W, cbw_ref, p, e0, e0 + COL_CHUNK))
        if g == 2:
            out_ref[:, c0:c0 + COL_CHUNK] = s.astype(out_ref.dtype)
            return
        scale = HEAD_DIM ** -0.5 if g == 0 else 1.0
        for h0 in range(0, COL_CHUNK, HEAD_DIM):
            sh = s[:, h0:h0 + HEAD_DIM]
            ss = jnp.sum(sh * sh, axis=-1, keepdims=True)
            out_ref[:, c0 + h0:c0 + h0 + HEAD_DIM] = (
                sh * (lax.rsqrt(ss + EPS) * scale)).astype(out_ref.dtype)

    def act_dots(off, c0):
        return (proj(off, c0),)

    def zb_epi(c0, p):
        zb_ref[:, c0:c0 + COL_CHUNK] = _silu(p).astype(zb_ref.dtype)

    def sgb_epi(c0, p):
        sgb_ref[:, c0:c0 + COL_CHUNK] = jax.nn.sigmoid(p).astype(sgb_ref.dtype)

    part = functools.partial
    cols = list(range(0, D_MODEL, COL_CHUNK))
    n_col = len(cols)
    stages = [[(part(qkv_dots, e0), part(qkv_epi, e0))] for e0 in range(0, QKV_W, COL_CHUNK)]
    for n, c0 in enumerate(cols):
        stages[2 * n].append((part(branch_a_dots, c0), part(branch_a_epi, c0)))
        stages[2 * n + 1] += [(part(act_dots, OFF_ZB, c0), part(zb_epi, c0)),
                              (part(act_dots, OFF_GB, c0), part(sgb_epi, c0))]
        stages[2 * n_col + n].append((part(gate_a_dots, c0), part(gate_a_epi, c0)))
    stages[2 * n_col].append((narrow_dots, narrow_epi))
    assert len(stages) == 3 * n_col

    def run_dots(i):
        widths, k = [], 0
        for dots, _ in stages[i]:
            outs = dots()
            for r in outs:
                stage_s[i % 2][k, :, 0:r.shape[1]] = r
                k += 1
            widths.append([r.shape[1] for r in outs])
        return widths

    widths = run_dots(0)
    for i, stage in enumerate(stages):
        ready = widths
        if i + 1 < len(stages):
            widths = run_dots(i + 1)
        k = 0
        for (_, epilogue), ws in zip(stage, ready):
            epilogue(*[stage_s[i % 2][k + n, :, 0:w] for n, w in enumerate(ws)])
            k += len(ws)
    if not has_state:
        sa_ref[...] = exta_s[ha + tm - na:ha + tm, :]
        sq_ref[...] = extb_s[hb + tm - nb:hb + tm, :]


def _resident(shape):
    nd = len(shape)
    return pl.BlockSpec(shape, lambda *_: (0,) * nd, pipeline_mode=pl.Buffered(1))


def _front(x2d, consts, *, tm, shift, n_seq, state=None):
    rows = x2d.shape[0]
    n_tiles = rows // tm
    tiles_per_seq = n_tiles // n_seq
    na = (CONV_A_W - 1) * shift
    nb = (CONV_B_W - 1) * shift
    hdr_a = -(-na // 8) * 8
    hdr_b = -(-nb // 8) * 8
    has_state = state is not None
    nw, wm, caw, cbw, hp, woa = consts

    row_spec = lambda w: pl.BlockSpec((tm, w), lambda i: (i, 0))
    in_specs = [row_spec(D_MODEL), _resident(nw.shape), _resident(wm.shape),
                _resident(caw.shape), _resident(cbw.shape), _resident(hp.shape),
                _resident(woa.shape)]
    args = [x2d, nw, wm, caw, cbw, hp, woa]
    out_specs = [row_spec(D_MODEL)] * 6 + [row_spec(NARROW)]
    out_shape = [jax.ShapeDtypeStruct((rows, D_MODEL), BF16)] * 6 + [
        jax.ShapeDtypeStruct((rows, NARROW), F32)]
    if has_state:
        sta, stq = state
        assert sta.shape[0] == hdr_a == na and stq.shape[0] == hdr_b == nb
        in_specs += [_resident(sta.shape), _resident(stq.shape)]
        args += [sta, stq]
        out_specs += [row_spec(D_MODEL), row_spec(QKV_W)]
        out_shape += [jax.ShapeDtypeStruct((rows, D_MODEL), F32),
                      jax.ShapeDtypeStruct((rows, QKV_W), F32)]
    else:
        seq_spec = lambda r, w: pl.BlockSpec((None, r, w), lambda i: (i // tiles_per_seq, 0, 0))
        out_specs += [seq_spec(na, D_MODEL), seq_spec(nb, QKV_W)]
        out_shape += [jax.ShapeDtypeStruct((n_seq, na, D_MODEL), F32),
                      jax.ShapeDtypeStruct((n_seq, nb, QKV_W), F32)]
    scratch = [pltpu.VMEM((tm, D_MODEL), BF16),
               pltpu.VMEM((hdr_a + tm, D_MODEL), F32),
               pltpu.VMEM((hdr_b + tm, QKV_W), F32),
               pltpu.VMEM((tm, D_MODEL), BF16),
               pltpu.VMEM((5, tm, COL_CHUNK), F32), pltpu.VMEM((5, tm, COL_CHUNK), F32)]
    return pl.pallas_call(
        functools.partial(_front_kernel, tm, shift, tiles_per_seq, has_state),
        grid=(n_tiles,),
        in_specs=in_specs,
        out_specs=out_specs,
        out_shape=out_shape,
        scratch_shapes=scratch,
        compiler_params=pltpu.CompilerParams(
            dimension_semantics=("arbitrary",), vmem_limit_bytes=V7X_VMEM_LIMIT),
        name="front",
    )(*args)


def _delta_kernel(chunk, cpb, spb, n_blocks, blocks_per_seq, has_init, fused_back, *refs):
    refs = list(refs)
    q_ref, k_ref, v_ref, nar_ref = refs[:4]
    del refs[:4]
    s0_ref = refs.pop(0) if has_init else None
    if fused_back:
        back_in = refs[:8]
        del refs[:8]
    o_ref, so_ref, st_s, up_s, wdq_s, kdt_s, aqk_s, dl_s, tp_s = refs[:9]
    C = chunk
    nch = spb * cpb
    levels = int(math.log2(C))
    grp = min(N_HEADS, 256 // C)
    W = grp * C
    sdt = wdq_s.dtype
    mx = lambda a: a.astype(BF16)
    step = pl.program_id(0)
    block = jnp.minimum(step, n_blocks - 1)
    fresh = (block % blocks_per_seq) == 0

    if fused_back:
        o_s, on_s, m_s = refs[9:]
        y_ref, o_dst = o_ref, o_s

        @pl.when(step == 0)
        def _():
            st_s[...] = jnp.zeros(st_s.shape, F32)
            o_s[...] = jnp.zeros(o_s.shape, F32)

        _back_kernel(back_in[0], o_s, *back_in[1:], y_ref, on_s, m_s)
    else:
        o_dst = o_ref

        @pl.when(step == 0)
        def _():
            st_s[...] = jnp.zeros(st_s.shape, F32)

    row = lax.broadcasted_iota(jnp.int32, (C, C), 0)
    col = lax.broadcasted_iota(jnp.int32, (C, C), 1)
    causal = row >= col
    strict = row > col
    ltri = causal.astype(BF16)
    utri = (row <= col).astype(BF16)
    rw = lax.broadcasted_iota(jnp.int32, (W, W), 0)
    cw = lax.broadcasted_iota(jnp.int32, (W, W), 1)
    blockdiag = (rw >> levels) == (cw >> levels)
    ri = lax.broadcasted_iota(jnp.int32, (C, W), 0)
    ci = lax.broadcasted_iota(jnp.int32, (C, W), 1)
    eye_cat = (ri == (ci & (C - 1))).astype(F32)
    groups = [(c, g) for c in range(nch) for g in range(N_HEADS // grp)]

    def cumsum_t(a, tri, dims):
        hi = a.astype(BF16)
        lo = (a - hi.astype(F32)).astype(BF16)
        dn = (dims, ((), ()))
        if dims[0] == (1,):
            return (lax.dot_general(tri, hi, dn, preferred_element_type=F32)
                    + lax.dot_general(tri, lo, dn, preferred_element_type=F32))
        return (lax.dot_general(hi, tri, dn, preferred_element_type=F32)
                + lax.dot_general(lo, tri, dn, preferred_element_type=F32))

    heads = [(c, h) for c in range(nch) for h in range(N_HEADS)]
    hsl = lambda h: slice(h * HEAD_DIM, (h + 1) * HEAD_DIM)
    rsl = lambda c: slice(c * C, (c + 1) * C)
    ident = (row == col).astype(BF16)
    gcs = [cumsum_t(nar_ref[rsl(c), :], ltri, ((1,), (0,))) for c in range(nch)]
    gcts = [cumsum_t(nar_ref[rsl(c), :], utri, ((0,), (0,))) for c in range(nch)]
    narts = [cumsum_t(nar_ref[rsl(c), :], ident, ((0,), (0,))) for c in range(nch)]
    a_kq = {}
    for c, h in heads:
        kx = mx(k_ref[rsl(c), hsl(h)])
        a_kq[c, h] = lax.dot_general(
            jnp.concatenate([kx, mx(q_ref[rsl(c), hsl(h)])], axis=0), kx,
            (((1,), (1,)), ((), ())), preferred_element_type=F32)
    for c in range(nch):
        dl_s[c] = jnp.exp(gcs[c][C - 1:C, :])
    for c, h in heads:
        g, j = divmod(h, grp)
        gcw = jnp.broadcast_to(gcs[c][:, N_HEADS + h:N_HEADS + h + 1], (C, HEAD_DIM))
        betw = jnp.broadcast_to(nar_ref[rsl(c), h:h + 1], (C, C))
        gcr = gcts[c][N_HEADS + h:N_HEADS + h + 1, :]
        decay = jnp.exp(jnp.where(causal, gcw[:, 0:C] - gcr, -jnp.inf))
        wdq_s[c, h, C:2 * C, :] = (q_ref[rsl(c), hsl(h)].astype(F32) * jnp.exp(gcw)).astype(sdt)
        k_t = k_ref[rsl(c), hsl(h)].astype(F32).T
        kdt_s[c, h] = (k_t * jnp.exp(gcr[:, C - 1:C] - gcr)).astype(sdt)
        a = a_kq[c, h]
        tp_s[c, g, C:2 * C, j * C:(j + 1) * C] = -jnp.where(strict, a[:C] * betw * decay, 0.0)
        aqk_s[c, h] = jnp.where(causal, a[C:] * decay, 0.0).astype(sdt)

    for c, g in groups:
        tp_s[c, g, 0:C, :] = eye_cat + tp_s[c, g, C:2 * C, :]
    for lv in range(levels):
        for c, g in groups:
            p = tp_s[c, g, C:2 * C, :]
            bd = mx(jnp.where(blockdiag, jnp.concatenate([p] * grp, axis=0), 0.0))
            if lv == 0:
                tp_s[c, g, C:2 * C, :] = _dot(mx(p), bd)
            elif lv < levels - 1:
                xx = _dot(mx(tp_s[c, g]), bd)
                tp_s[c, g, 0:C, :] = tp_s[c, g, 0:C, :] + xx[:C]
                tp_s[c, g, C:2 * C, :] = xx[C:]
            else:
                t = tp_s[c, g, 0:C, :]
                tp_s[c, g, 0:C, :] = t + _dot(mx(t), bd)

    for c, h in heads:
        g, j = divmod(h, grp)
        tb = tp_s[c, g, 0:C, j * C:(j + 1) * C] * narts[c][h:h + 1, :]
        tbe = tb * jnp.exp(gcts[c][N_HEADS + h:N_HEADS + h + 1, :])
        up_s[c, h] = _dot(mx(tb), mx(v_ref[rsl(c), hsl(h)]))
        wdq_s[c, h, 0:C, :] = _dot(mx(tbe), mx(k_ref[rsl(c), hsl(h)])).astype(sdt)

    init = s0_ref[...] if has_init else jnp.zeros(st_s.shape, F32)
    st_s[...] = jnp.where(fresh, init, st_s[...])
    for cc in range(cpb):
        chains = [(s * cpb + cc, s, h) for s in range(spb) for h in range(N_HEADS)]
        rs = [_dot(mx(wdq_s[c, h]), mx(st_s[s, h])) for c, s, h in chains]
        us = [mx(up_s[c, h] - r[:C]) for (c, s, h), r in zip(chains, rs)]
        for (c, s, h), r, u in zip(chains, rs, us):
            o = r[C:] + _dot(mx(aqk_s[c, h]), u)
            o_dst[c * C:(c + 1) * C, h * HEAD_DIM:(h + 1) * HEAD_DIM] = o.astype(o_dst.dtype)
        for (c, s, h), u in zip(chains, us):
            dl = dl_s[c][:, N_HEADS + h:N_HEADS + h + 1]
            new = st_s[s, h] * dl + _dot(mx(kdt_s[c, h]), u)
            st_s[s, h] = jnp.where(step < n_blocks, new, st_s[s, h]) if fused_back else new

    so_ref[...] = st_s[...]


def _delta(q, k, v, nar, *, n_seq, chunk, cpb, spb=1, s0=None, back=None):
    rows = q.shape[0]
    seq_len = rows // n_seq
    tb = chunk * cpb
    blocks_per_seq = seq_len // tb
    assert spb == 1 or blocks_per_seq == 1
    n_blocks = (n_seq // spb) * blocks_per_seq
    has_init = s0 is not None
    fused_back = back is not None
    nch = spb * cpb
    cur = lambda s: jnp.minimum(s, n_blocks - 1)
    prev = lambda s: jnp.maximum(s - 1, 0)
    row_spec = lambda w, at: pl.BlockSpec((spb * tb, w), lambda s: (at(s), 0))
    st_shape = (spb, N_HEADS, HEAD_DIM, HEAD_DIM)
    in_specs = [row_spec(D_MODEL, cur)] * 3 + [row_spec(NARROW, cur)]
    args = [q, k, v, nar]
    if has_init:
        in_specs.append(pl.BlockSpec(st_shape, lambda s: (cur(s), 0, 0, 0)))
        args.append(s0)
    sdt = BF16 if chunk % 16 == 0 else F32
    grp = min(N_HEADS, 256 // chunk)
    scratch = [pltpu.VMEM(st_shape, F32),
               pltpu.VMEM((nch, N_HEADS, chunk, HEAD_DIM), F32),
               pltpu.VMEM((nch, N_HEADS, 2 * chunk, HEAD_DIM), sdt),
               pltpu.VMEM((nch, N_HEADS, HEAD_DIM, chunk), sdt),
               pltpu.VMEM((nch, N_HEADS, chunk, chunk), sdt),
               pltpu.VMEM((nch, 1, NARROW), F32),
               pltpu.VMEM((nch, N_HEADS // grp, 2 * chunk, grp * chunk), F32)]
    if fused_back:
        x2d, zb, ma, sgb, consts = back
        in_specs += [row_spec(D_MODEL, prev)] * 4 + [_resident(c.shape) for c in consts]
        args += [x2d, zb, ma, sgb, *consts]
        out0 = (row_spec(D_MODEL, prev), jax.ShapeDtypeStruct((rows, D_MODEL), F32))
        scratch += [pltpu.VMEM((spb * tb, D_MODEL), F32),
                    pltpu.VMEM((spb * tb, D_MODEL), BF16), pltpu.VMEM((spb * tb, D_MODEL), BF16)]
    else:
        out0 = (row_spec(D_MODEL, cur), jax.ShapeDtypeStruct((rows, D_MODEL), q.dtype))
    return pl.pallas_call(
        functools.partial(_delta_kernel, chunk, cpb, spb, n_blocks, blocks_per_seq, has_init,
                          fused_back),
        grid=(n_blocks + 1 if fused_back else n_blocks,),
        in_specs=in_specs,
        out_specs=[out0[0],
                   pl.BlockSpec(st_shape, lambda s: (cur(s) // blocks_per_seq, 0, 0, 0))],
        out_shape=[out0[1], jax.ShapeDtypeStruct((n_seq, N_HEADS, HEAD_DIM, HEAD_DIM), F32)],
        scratch_shapes=scratch,
        compiler_params=pltpu.CompilerParams(
            dimension_semantics=("arbitrary",), vmem_limit_bytes=V7X_VMEM_LIMIT),
        name="delta",
    )(*args)


def _back_kernel(x_ref, o_ref, zb_ref, ma_ref, sgb_ref, onw_ref, wob_ref, wo_ref, fw_ref,
                 y_ref, on_s, m_s):
    for h in range(N_HEADS):
        hs = slice(h * HEAD_DIM, (h + 1) * HEAD_DIM)
        oh = o_ref[:, hs].astype(F32)
        ms = jnp.mean(oh * oh, axis=-1, keepdims=True)
        on = oh * lax.rsqrt(ms + EPS) * onw_ref[...]
        on_s[:, hs] = (on * zb_ref[:, hs].astype(F32)).astype(BF16)
    for c0 in range(0, D_MODEL, COL_CHUNK):
        c1 = c0 + COL_CHUNK
        yb = _dot(on_s[...], wob_ref[:, c0:c1])
        m = ma_ref[:, c0:c1].astype(F32) + sgb_ref[:, c0:c1].astype(F32) * yb
        m_s[:, c0:c1] = m.astype(BF16)
    hres = x_ref[...] + _dot(m_s[...], wo_ref[...])
    var = jnp.mean(hres * hres, axis=-1, keepdims=True)
    y_ref[...] = hres * lax.rsqrt(var + EPS) * fw_ref[...]


def _back(x2d, o, zb, ma, sgb, consts, *, tm):
    rows = x2d.shape[0]
    onw, wob, wo, fw = consts
    row_spec = pl.BlockSpec((tm, D_MODEL), lambda i: (i, 0))
    return pl.pallas_call(
        _back_kernel,
        grid=(rows // tm,),
        in_specs=[row_spec] * 5 + [_resident(onw.shape), _resident(wob.shape),
                                   _resident(wo.shape), _resident(fw.shape)],
        out_specs=row_spec,
        out_shape=jax.ShapeDtypeStruct((rows, D_MODEL), F32),
        scratch_shapes=[pltpu.VMEM((tm, D_MODEL), BF16), pltpu.VMEM((tm, D_MODEL), BF16)],
        compiler_params=pltpu.CompilerParams(
            dimension_semantics=("arbitrary",), vmem_limit_bytes=V7X_VMEM_LIMIT),
        name="back",
    )(x2d, o, zb, ma, sgb, onw, wob, wo, fw)


def kernel(x_prompt, x_sample, state_conv_a, state_conv_qkv, state_delta, w_in, conv_a_w,
           conv_b_w, a_log, dt_bias, onorm_w, w_out_a, w_out_b, w_o, norm_w, final_norm_w):
    assert w_in.shape[0] == 1, "single layer"
    bp, tp, _ = x_prompt.shape
    bs, ts, _ = x_sample.shape
    w = w_in[0]
    n_wide = OFF_ZB + D_MODEL
    wn = jnp.pad(w[:, n_wide:n_wide + 2 * N_HEADS], ((0, 0), (0, NARROW - 2 * N_HEADS)))
    wm = jnp.concatenate([w[:, :n_wide], w[:, n_wide + 2 * N_HEADS:], wn], axis=1).astype(BF16)
    hp = jnp.zeros((8, NARROW), F32)
    hp = hp.at[0, N_HEADS:2 * N_HEADS].set(a_log[0]).at[1, N_HEADS:2 * N_HEADS].set(dt_bias[0])
    front_consts = (norm_w[0][None, :], wm, conv_a_w[0], conv_b_w[0], hp,
                    w_out_a[0].astype(BF16))
    back_consts = (onorm_w[0][None, :], w_out_b[0].astype(BF16), w_o[0].astype(BF16),
                   final_norm_w[None, :])

    xp = x_prompt.reshape(bp * tp, D_MODEL)
    ma, q, k, v, zb, sgb, nar, sa, sq = _front(xp, front_consts, tm=256, shift=1, n_seq=bp)
    y_prompt, s_new = _delta(q, k, v, nar, n_seq=bp, chunk=PROMPT_CHUNK, cpb=4,
                             back=(xp, zb, ma, sgb, back_consts))
    y_prompt = y_prompt.reshape(bp, tp, D_MODEL)
    new_conv_a_prompt = sa[None]
    new_conv_qkv_prompt = sq[None]
    new_delta_prompt = s_new[None]

    xs = jnp.transpose(x_sample, (1, 0, 2)).reshape(ts * bs, D_MODEL)
    sta = jnp.transpose(state_conv_a[0], (1, 0, 2)).reshape((CONV_A_W - 1) * bs, D_MODEL)
    stq = jnp.transpose(state_conv_qkv[0], (1, 0, 2)).reshape((CONV_B_W - 1) * bs, QKV_W)
    ma, q, k, v, zb, sgb, nar, cha, pqkv = _front(
        xs, front_consts, tm=bs, shift=bs, n_seq=1, state=(sta, stq))
    sa = jnp.concatenate([sta, cha], axis=0)[ts * bs:]
    sq = jnp.concatenate([stq, pqkv], axis=0)[ts * bs:]
    tpad = 8

    def to_batch_major(a):
        a = jnp.transpose(a.reshape(ts, bs, a.shape[-1]), (1, 0, 2)).astype(F32)
        return jnp.pad(a, ((0, 0), (0, tpad - ts), (0, 0))).reshape(bs * tpad, a.shape[-1])

    o, s_new = _delta(to_batch_major(q), to_batch_major(k), to_batch_major(v),
                      to_batch_major(nar), n_seq=bs, chunk=tpad, cpb=1, spb=8, s0=state_delta[0])
    o = jnp.transpose(o.reshape(bs, tpad, D_MODEL)[:, :ts], (1, 0, 2)).reshape(ts * bs, D_MODEL)
    ys = _back(xs, o, zb, ma, sgb, back_consts, tm=ts * bs)
    y_sample = jnp.transpose(ys.reshape(ts, bs, D_MODEL), (1, 0, 2))
    new_conv_a_sample = jnp.transpose(sa.reshape(CONV_A_W - 1, bs, D_MODEL), (1, 0, 2))[None]
    new_conv_qkv_sample = jnp.transpose(sq.reshape(CONV_B_W - 1, bs, QKV_W), (1, 0, 2))[None]
    new_delta_sample = s_new[None]

    return (y_prompt, y_sample, new_conv_a_prompt, new_conv_qkv_prompt, new_delta_prompt,
            new_conv_a_sample, new_conv_qkv_sample, new_delta_sample)
```

```python
import functools
import math

import jax
import jax.numpy as jnp
from jax import lax
from jax.experimental import pallas as pl
from jax.experimental.pallas import tpu as pltpu

D_MODEL = 1024
N_HEADS = 8
HEAD_DIM = 128
QKV_W = 3 * D_MODEL
CONV_A_W = 3
CONV_B_W = 4
PROMPT_CHUNK = 64
EPS = 1e-6
NARROW = 128
OFF_B, OFF_C, OFF_H, OFF_Z, OFF_QKV, OFF_ZB, OFF_GA, OFF_GB, OFF_NARROW = (
    0, 1024, 2048, 3072, 4096, 7168, 8192, 9216, 10240)

V7X_VMEM_LIMIT = 56 * 1024 * 1024
COL_CHUNK = 256

F32 = jnp.float32
BF16 = jnp.bfloat16


def _dot(a, b):
    return jnp.dot(a, b, preferred_element_type=F32)


def _silu(x):
    return x * jax.nn.sigmoid(x)


def _softplus(x):
    return jnp.maximum(x, 0.0) + jnp.log(1.0 + jnp.exp(-jnp.abs(x)))


def _front_kernel(tm, shift, tiles_per_seq, has_state, *refs):
    if has_state:
        (x_ref, nw_ref, wm_ref, wt_ref, caw_ref, cbw_ref, hp_ref, woa_ref, sta_ref, stq_ref,
         ma_ref, q_ref, k_ref, v_ref, zb_ref, sgb_ref, nar_ref, sa_ref, sq_ref,
         xn_s, exta_s, extb_s, pre_s, *stage_s) = refs
    else:
        (x_ref, nw_ref, wm_ref, wt_ref, caw_ref, cbw_ref, hp_ref, woa_ref,
         ma_ref, q_ref, k_ref, v_ref, zb_ref, sgb_ref, nar_ref, sa_ref, sq_ref,
         xn_s, exta_s, extb_s, pre_s, *stage_s) = refs
        sta_ref = stq_ref = None
    ha = exta_s.shape[0] - tm
    hb = extb_s.shape[0] - tm
    na = (CONV_A_W - 1) * shift
    nb = (CONV_B_W - 1) * shift
    first = (pl.program_id(0) % tiles_per_seq) == 0

    x = x_ref[...]
    var = jnp.mean(x * x, axis=-1, keepdims=True)
    xn_s[...] = (x * lax.rsqrt(var + EPS) * nw_ref[...]).astype(BF16)

    def proj(off, c0, width=COL_CHUNK):
        w_ref, lo = (wm_ref, off + c0) if off < OFF_GA else (wt_ref, off - OFF_GA + c0)
        return _dot(xn_s[...], w_ref[:, lo:lo + width])

    @pl.when(pl.program_id(0) == 0)
    def _():
        exta_s[tm:tm + ha, :] = jnp.zeros((ha, D_MODEL), F32)
        extb_s[tm:tm + hb, :] = jnp.zeros((hb, QKV_W), F32)

    for ext_s, st_ref, hrows in ((exta_s, sta_ref, ha), (extb_s, stq_ref, hb)):
        for c0 in range(0, ext_s.shape[1], COL_CHUNK):
            cs = slice(c0, c0 + COL_CHUNK)
            init = st_ref[:, cs] if has_state else jnp.zeros((hrows, COL_CHUNK), F32)
            ext_s[0:hrows, cs] = jnp.where(first, init, ext_s[tm:tm + hrows, cs])

    def causal_conv(ext_s, hrows, width, w_ref, cur, c0, c1):
        ext_s[hrows:hrows + tm, c0:c1] = cur
        conv = cur * w_ref[width - 1:width, c0:c1]
        for j in range(width - 1):
            d = (width - 1 - j) * shift
            conv = conv + ext_s[hrows - d:hrows - d + tm, c0:c1] * w_ref[j:j + 1, c0:c1]
        return conv

    def narrow_dots():
        return (proj(OFF_NARROW, 0, NARROW),)

    def narrow_epi(nar):
        lane = lax.broadcasted_iota(jnp.int32, nar.shape, 1)
        beta = jax.nn.sigmoid(nar)
        gdec = -jnp.exp(hp_ref[0:1, :]) * _softplus(nar + hp_ref[1:2, :])
        nar_ref[...] = jnp.where(lane < N_HEADS, beta, jnp.where(lane < 2 * N_HEADS, gdec, 0.0))

    def branch_a_dots(c0):
        return tuple(proj(off, c0) for off in (OFF_B, OFF_C, OFF_H, OFF_Z))

    def branch_a_epi(c0, pb, pc, ph, pz):
        c1 = c0 + COL_CHUNK
        ch = pc * ph
        if has_state:
            sa_ref[:, c0:c1] = ch
        conv = causal_conv(exta_s, ha, CONV_A_W, caw_ref, ch, c0, c1)
        pre_s[:, c0:c1] = (_silu(pz) * pb * conv).astype(BF16)

    def gate_a_dots(c0):
        return (_dot(pre_s[...], woa_ref[:, c0:c0 + COL_CHUNK]), proj(OFF_GA, c0))

    def gate_a_epi(c0, ya, ga):
        ma_ref[:, c0:c0 + COL_CHUNK] = (jax.nn.sigmoid(ga) * ya).astype(ma_ref.dtype)

    def qkv_dots(e0):
        return (proj(OFF_QKV, e0),)

    def qkv_epi(e0, p):
        g, c0 = divmod(e0, D_MODEL)
        out_ref = (q_ref, k_ref, v_ref)[g]
        if has_state:
            sq_ref[:, e0:e0 + COL_CHUNK] = p
        s = _silu(causal_conv(extb_s, hb, CONV_B_W, cbw_ref, p, e0, e0 + COL_CHUNK))
        if g == 2:
            out_ref[:, c0:c0 + COL_CHUNK] = s.astype(out_ref.dtype)
            return
        scale = HEAD_DIM ** -0.5 if g == 0 else 1.0
        for h0 in range(0, COL_CHUNK, HEAD_DIM):
            sh = s[:, h0:h0 + HEAD_DIM]
            ss = jnp.sum(sh * sh, axis=-1, keepdims=True)
            out_ref[:, c0 + h0:c0 + h0 + HEAD_DIM] = (
                sh * (lax.rsqrt(ss + EPS) * scale)).astype(out_ref.dtype)

    def act_dots(off, c0):
        return (proj(off, c0),)

    def zb_epi(c0, p):
        zb_ref[:, c0:c0 + COL_CHUNK] = _silu(p).astype(zb_ref.dtype)

    def sgb_epi(c0, p):
        sgb_ref[:, c0:c0 + COL_CHUNK] = jax.nn.sigmoid(p).astype(sgb_ref.dtype)

    part = functools.partial
    cols = list(range(0, D_MODEL, COL_CHUNK))
    n_col = len(cols)
    stages = [[(part(qkv_dots, e0), part(qkv_epi, e0))] for e0 in range(0, QKV_W, COL_CHUNK)]
    for n, c0 in enumerate(cols):
        stages[2 * n].append((part(branch_a_dots, c0), part(branch_a_epi, c0)))
        stages[2 * n + 1] += [(part(act_dots, OFF_ZB, c0), part(zb_epi, c0)),
                              (part(act_dots, OFF_GB, c0), part(sgb_epi, c0))]
        stages[2 * n_col + n].append((part(gate_a_dots, c0), part(gate_a_epi, c0)))
    stages[2 * n_col].append((narrow_dots, narrow_epi))
    assert len(stages) == 3 * n_col

    def run_dots(i):
        widths, k = [], 0
        for dots, _ in stages[i]:
            outs = dots()
            for r in outs:
                stage_s[i % 2][k, :, 0:r.shape[1]] = r
                k += 1
            widths.append([r.shape[1] for r in outs])
        return widths

    widths = run_dots(0)
    for i, stage in enumerate(stages):
        ready = widths
        if i + 1 < len(stages):
            widths = run_dots(i + 1)
        k = 0
        for (_, epilogue), ws in zip(stage, ready):
            epilogue(*[stage_s[i % 2][k + n, :, 0:w] for n, w in enumerate(ws)])
            k += len(ws)
    if not has_state:
        sa_ref[...] = exta_s[ha + tm - na:ha + tm, :]
        sq_ref[...] = extb_s[hb + tm - nb:hb + tm, :]


def _resident(shape):
    nd = len(shape)
    return pl.BlockSpec(shape, lambda *_: (0,) * nd, pipeline_mode=pl.Buffered(1))


def _front(x2d, consts, *, tm, shift, n_seq, state=None):
    rows = x2d.shape[0]
    n_tiles = rows // tm
    tiles_per_seq = n_tiles // n_seq
    na = (CONV_A_W - 1) * shift
    nb = (CONV_B_W - 1) * shift
    hdr_a = -(-na // 8) * 8
    hdr_b = -(-nb // 8) * 8
    has_state = state is not None
    nw, wm, wt, caw, cbw, hp, woa = consts

    row_spec = lambda w: pl.BlockSpec((tm, w), lambda i: (i, 0))
    in_specs = [row_spec(D_MODEL), _resident(nw.shape), _resident(wm.shape), _resident(wt.shape),
                _resident(caw.shape), _resident(cbw.shape), _resident(hp.shape),
                _resident(woa.shape)]
    args = [x2d, nw, wm, wt, caw, cbw, hp, woa]
    out_specs = [row_spec(D_MODEL)] * 6 + [row_spec(NARROW)]
    out_shape = [jax.ShapeDtypeStruct((rows, D_MODEL), BF16)] * 6 + [
        jax.ShapeDtypeStruct((rows, NARROW), F32)]
    if has_state:
        sta, stq = state
        assert sta.shape[0] == hdr_a == na and stq.shape[0] == hdr_b == nb
        in_specs += [_resident(sta.shape), _resident(stq.shape)]
        args += [sta, stq]
        out_specs += [row_spec(D_MODEL), row_spec(QKV_W)]
        out_shape += [jax.ShapeDtypeStruct((rows, D_MODEL), F32),
                      jax.ShapeDtypeStruct((rows, QKV_W), F32)]
    else:
        seq_spec = lambda r, w: pl.BlockSpec((None, r, w), lambda i: (i // tiles_per_seq, 0, 0))
        out_specs += [seq_spec(na, D_MODEL), seq_spec(nb, QKV_W)]
        out_shape += [jax.ShapeDtypeStruct((n_seq, na, D_MODEL), F32),
                      jax.ShapeDtypeStruct((n_seq, nb, QKV_W), F32)]
    scratch = [pltpu.VMEM((tm, D_MODEL), BF16),
               pltpu.VMEM((hdr_a + tm, D_MODEL), F32),
               pltpu.VMEM((hdr_b + tm, QKV_W), F32),
               pltpu.VMEM((tm, D_MODEL), BF16),
               pltpu.VMEM((5, tm, COL_CHUNK), F32), pltpu.VMEM((5, tm, COL_CHUNK), F32)]
    return pl.pallas_call(
        functools.partial(_front_kernel, tm, shift, tiles_per_seq, has_state),
        grid=(n_tiles,),
        in_specs=in_specs,
        out_specs=out_specs,
        out_shape=out_shape,
        scratch_shapes=scratch,
        compiler_params=pltpu.CompilerParams(
            dimension_semantics=("arbitrary",), vmem_limit_bytes=V7X_VMEM_LIMIT),
        name="front",
    )(*args)


def _delta_kernel(chunk, cpb, spb, n_blocks, blocks_per_seq, has_init, fused_back, *refs):
    refs = list(refs)
    q_ref, k_ref, v_ref, nar_ref = refs[:4]
    del refs[:4]
    s0_ref = refs.pop(0) if has_init else None
    if fused_back:
        back_in = refs[:8]
        del refs[:8]
    o_ref, so_ref, st_s, up_s, wdq_s, kdt_s, aqk_s, dl_s, tp_s = refs[:9]
    C = chunk
    nch = spb * cpb
    levels = int(math.log2(C))
    grp = min(N_HEADS, 256 // C)
    W = grp * C
    sdt = wdq_s.dtype
    mx = lambda a: a.astype(BF16)
    step = pl.program_id(0)
    block = jnp.minimum(step, n_blocks - 1)
    fresh = (block % blocks_per_seq) == 0

    if fused_back:
        o_s, on_s, m_s = refs[9:]
        y_ref, o_dst = o_ref, o_s

        @pl.when(step == 0)
        def _():
            st_s[...] = jnp.zeros(st_s.shape, F32)
            o_s[...] = jnp.zeros(o_s.shape, F32)

        _back_kernel(back_in[0], o_s, *back_in[1:], y_ref, on_s, m_s)
    else:
        o_dst = o_ref

        @pl.when(step == 0)
        def _():
            st_s[...] = jnp.zeros(st_s.shape, F32)

    row = lax.broadcasted_iota(jnp.int32, (C, C), 0)
    col = lax.broadcasted_iota(jnp.int32, (C, C), 1)
    causal = row >= col
    strict = row > col
    ltri = causal.astype(BF16)
    utri = (row <= col).astype(BF16)
    rw = lax.broadcasted_iota(jnp.int32, (W, W), 0)
    cw = lax.broadcasted_iota(jnp.int32, (W, W), 1)
    blockdiag = (rw >> levels) == (cw >> levels)
    ri = lax.broadcasted_iota(jnp.int32, (C, W), 0)
    ci = lax.broadcasted_iota(jnp.int32, (C, W), 1)
    eye_cat = (ri == (ci & (C - 1))).astype(F32)
    groups = [(c, g) for c in range(nch) for g in range(N_HEADS // grp)]

    def cumsum_t(a, tri, dims):
        hi = a.astype(BF16)
        lo = (a - hi.astype(F32)).astype(BF16)
        dn = (dims, ((), ()))
        if dims[0] == (1,):
            return (lax.dot_general(tri, hi, dn, preferred_element_type=F32)
                    + lax.dot_general(tri, lo, dn, preferred_element_type=F32))
        return (lax.dot_general(hi, tri, dn, preferred_element_type=F32)
                + lax.dot_general(lo, tri, dn, preferred_element_type=F32))

    heads = [(c, h) for c in range(nch) for h in range(N_HEADS)]
    hsl = lambda h: slice(h * HEAD_DIM, (h + 1) * HEAD_DIM)
    rsl = lambda c: slice(c * C, (c + 1) * C)
    ident = (row == col).astype(BF16)
    gcs = [cumsum_t(nar_ref[rsl(c), :], ltri, ((1,), (0,))) for c in range(nch)]
    gcts = [cumsum_t(nar_ref[rsl(c), :], utri, ((0,), (0,))) for c in range(nch)]
    narts = [cumsum_t(nar_ref[rsl(c), :], ident, ((0,), (0,))) for c in range(nch)]
    a_kq = {}
    for c, h in heads:
        kx = mx(k_ref[rsl(c), hsl(h)])
        a_kq[c, h] = lax.dot_general(
            jnp.concatenate([kx, mx(q_ref[rsl(c), hsl(h)])], axis=0), kx,
            (((1,), (1,)), ((), ())), preferred_element_type=F32)
    for c in range(nch):
        dl_s[c] = jnp.exp(gcs[c][C - 1:C, :])
    for c, h in heads:
        g, j = divmod(h, grp)
        gcw = jnp.broadcast_to(gcs[c][:, N_HEADS + h:N_HEADS + h + 1], (C, HEAD_DIM))
        betw = jnp.broadcast_to(nar_ref[rsl(c), h:h + 1], (C, C))
        gcr = gcts[c][N_HEADS + h:N_HEADS + h + 1, :]
        decay = jnp.exp(jnp.where(causal, gcw[:, 0:C] - gcr, -jnp.inf))
        wdq_s[c, h, C:2 * C, :] = (q_ref[rsl(c), hsl(h)].astype(F32) * jnp.exp(gcw)).astype(sdt)
        k_t = k_ref[rsl(c), hsl(h)].astype(F32).T
        kdt_s[c, h] = (k_t * jnp.exp(gcr[:, C - 1:C] - gcr)).astype(sdt)
        a = a_kq[c, h]
        tp_s[c, g, C:2 * C, j * C:(j + 1) * C] = -jnp.where(strict, a[:C] * betw * decay, 0.0)
        aqk_s[c, h] = jnp.where(causal, a[C:] * decay, 0.0).astype(sdt)

    for c, g in groups:
        tp_s[c, g, 0:C, :] = eye_cat + tp_s[c, g, C:2 * C, :]
    for lv in range(levels):
        for c, g in groups:
            p = tp_s[c, g, C:2 * C, :]
            bd = mx(jnp.where(blockdiag, jnp.concatenate([p] * grp, axis=0), 0.0))
            if lv == 0:
                tp_s[c, g, C:2 * C, :] = _dot(mx(p), bd)
            elif lv < levels - 1:
                xx = _dot(mx(tp_s[c, g]), bd)
                tp_s[c, g, 0:C, :] = tp_s[c, g, 0:C, :] + xx[:C]
                tp_s[c, g, C:2 * C, :] = xx[C:]
            else:
                t = tp_s[c, g, 0:C, :]
                tp_s[c, g, 0:C, :] = t + _dot(mx(t), bd)

    for c, h in heads:
        g, j = divmod(h, grp)
        tb = tp_s[c, g, 0:C, j * C:(j + 1) * C] * narts[c][h:h + 1, :]
        tbe = tb * jnp.exp(gcts[c][N_HEADS + h:N_HEADS + h + 1, :])
        up_s[c, h] = _dot(mx(tb), mx(v_ref[rsl(c), hsl(h)]))
        wdq_s[c, h, 0:C, :] = _dot(mx(tbe), mx(k_ref[rsl(c), hsl(h)])).astype(sdt)

    init = s0_ref[...] if has_init else jnp.zeros(st_s.shape, F32)
    st_s[...] = jnp.where(fresh, init, st_s[...])
    for cc in range(cpb):
        chains = [(s * cpb + cc, s, h) for s in range(spb) for h in range(N_HEADS)]
        rs = [_dot(mx(wdq_s[c, h]), mx(st_s[s, h])) for c, s, h in chains]
        us = [mx(up_s[c, h] - r[:C]) for (c, s, h), r in zip(chains, rs)]
        for (c, s, h), r, u in zip(chains, rs, us):
            o = r[C:] + _dot(mx(aqk_s[c, h]), u)
            o_dst[c * C:(c + 1) * C, h * HEAD_DIM:(h + 1) * HEAD_DIM] = o.astype(o_dst.dtype)
        for (c, s, h), u in zip(chains, us):
            dl = dl_s[c][:, N_HEADS + h:N_HEADS + h + 1]
            new = st_s[s, h] * dl + _dot(mx(kdt_s[c, h]), u)
            st_s[s, h] = jnp.where(step < n_blocks, new, st_s[s, h]) if fused_back else new

    so_ref[...] = st_s[...]


def _delta(q, k, v, nar, *, n_seq, chunk, cpb, spb=1, s0=None, back=None):
    rows = q.shape[0]
    seq_len = rows // n_seq
    tb = chunk * cpb
    blocks_per_seq = seq_len // tb
    assert spb == 1 or blocks_per_seq == 1
    n_blocks = (n_seq // spb) * blocks_per_seq
    has_init = s0 is not None
    fused_back = back is not None
    nch = spb * cpb
    cur = lambda s: jnp.minimum(s, n_blocks - 1)
    prev = lambda s: jnp.maximum(s - 1, 0)
    row_spec = lambda w, at: pl.BlockSpec((spb * tb, w), lambda s: (at(s), 0))
    st_shape = (spb, N_HEADS, HEAD_DIM, HEAD_DIM)
    in_specs = [row_spec(D_MODEL, cur)] * 3 + [row_spec(NARROW, cur)]
    args = [q, k, v, nar]
    if has_init:
        in_specs.append(pl.BlockSpec(st_shape, lambda s: (cur(s), 0, 0, 0)))
        args.append(s0)
    sdt = BF16 if chunk % 16 == 0 else F32
    grp = min(N_HEADS, 256 // chunk)
    scratch = [pltpu.VMEM(st_shape, F32),
               pltpu.VMEM((nch, N_HEADS, chunk, HEAD_DIM), F32),
               pltpu.VMEM((nch, N_HEADS, 2 * chunk, HEAD_DIM), sdt),
               pltpu.VMEM((nch, N_HEADS, HEAD_DIM, chunk), sdt),
               pltpu.VMEM((nch, N_HEADS, chunk, chunk), sdt),
               pltpu.VMEM((nch, 1, NARROW), F32),
               pltpu.VMEM((nch, N_HEADS // grp, 2 * chunk, grp * chunk), F32)]
    if fused_back:
        x2d, zb, ma, sgb, consts = back
        in_specs += [row_spec(D_MODEL, prev)] * 4 + [_resident(c.shape) for c in consts]
        args += [x2d, zb, ma, sgb, *consts]
        out0 = (row_spec(D_MODEL, prev), jax.ShapeDtypeStruct((rows, D_MODEL), F32))
        scratch += [pltpu.VMEM((spb * tb, D_MODEL), F32),
                    pltpu.VMEM((spb * tb, D_MODEL), BF16), pltpu.VMEM((spb * tb, D_MODEL), BF16)]
    else:
        out0 = (row_spec(D_MODEL, cur), jax.ShapeDtypeStruct((rows, D_MODEL), q.dtype))
    return pl.pallas_call(
        functools.partial(_delta_kernel, chunk, cpb, spb, n_blocks, blocks_per_seq, has_init,
                          fused_back),
        grid=(n_blocks + 1 if fused_back else n_blocks,),
        in_specs=in_specs,
        out_specs=[out0[0],
                   pl.BlockSpec(st_shape, lambda s: (cur(s) // blocks_per_seq, 0, 0, 0))],
        out_shape=[out0[1], jax.ShapeDtypeStruct((n_seq, N_HEADS, HEAD_DIM, HEAD_DIM), F32)],
        scratch_shapes=scratch,
        compiler_params=pltpu.CompilerParams(
            dimension_semantics=("arbitrary",), vmem_limit_bytes=V7X_VMEM_LIMIT),
        name="delta",
    )(*args)


def _back_kernel(x_ref, o_ref, zb_ref, ma_ref, sgb_ref, onw_ref, wob_ref, wo_ref, fw_ref,
                 y_ref, on_s, m_s):
    for h in range(N_HEADS):
        hs = slice(h * HEAD_DIM, (h + 1) * HEAD_DIM)
        oh = o_ref[:, hs].astype(F32)
        ms = jnp.mean(oh * oh, axis=-1, keepdims=True)
        on = oh * lax.rsqrt(ms + EPS) * onw_ref[...]
        on_s[:, hs] = (on * zb_ref[:, hs].astype(F32)).astype(BF16)
    for c0 in range(0, D_MODEL, COL_CHUNK):
        c1 = c0 + COL_CHUNK
        yb = _dot(on_s[...], wob_ref[:, c0:c1])
        m = ma_ref[:, c0:c1].astype(F32) + sgb_ref[:, c0:c1].astype(F32) * yb
        m_s[:, c0:c1] = m.astype(BF16)
    hres = x_ref[...] + _dot(m_s[...], wo_ref[...])
    var = jnp.mean(hres * hres, axis=-1, keepdims=True)
    y_ref[...] = hres * lax.rsqrt(var + EPS) * fw_ref[...]


def _back(x2d, o, zb, ma, sgb, consts, *, tm):
    rows = x2d.shape[0]
    onw, wob, wo, fw = consts
    row_spec = pl.BlockSpec((tm, D_MODEL), lambda i: (i, 0))
    return pl.pallas_call(
        _back_kernel,
        grid=(rows // tm,),
        in_specs=[row_spec] * 5 + [_resident(onw.shape), _resident(wob.shape),
                                   _resident(wo.shape), _resident(fw.shape)],
        out_specs=row_spec,
        out_shape=jax.ShapeDtypeStruct((rows, D_MODEL), F32),
        scratch_shapes=[pltpu.VMEM((tm, D_MODEL), BF16), pltpu.VMEM((tm, D_MODEL), BF16)],
        compiler_params=pltpu.CompilerParams(
            dimension_semantics=("arbitrary",), vmem_limit_bytes=V7X_VMEM_LIMIT),
        name="back",
    )(x2d, o, zb, ma, sgb, onw, wob, wo, fw)


def kernel(x_prompt, x_sample, state_conv_a, state_conv_qkv, state_delta, w_in, conv_a_w,
           conv_b_w, a_log, dt_bias, onorm_w, w_out_a, w_out_b, w_o, norm_w, final_norm_w):
    assert w_in.shape[0] == 1, "single layer"
    bp, tp, _ = x_prompt.shape
    bs, ts, _ = x_sample.shape
    wm = w_in[0].astype(BF16)
    n_wide = OFF_ZB + D_MODEL
    wn = jnp.pad(wm[:, n_wide:n_wide + 2 * N_HEADS], ((0, 0), (0, NARROW - 2 * N_HEADS)))
    wt = jnp.concatenate([wm[:, n_wide + 2 * N_HEADS:], wn], axis=1)
    hp = jnp.zeros((8, NARROW), F32)
    hp = hp.at[0, N_HEADS:2 * N_HEADS].set(a_log[0]).at[1, N_HEADS:2 * N_HEADS].set(dt_bias[0])
    front_consts = (norm_w[0][None, :], wm, wt, conv_a_w[0], conv_b_w[0], hp,
                    w_out_a[0].astype(BF16))
    back_consts = (onorm_w[0][None, :], w_out_b[0].astype(BF16), w_o[0].astype(BF16),
                   final_norm_w[None, :])

    xp = x_prompt.reshape(bp * tp, D_MODEL)
    ma, q, k, v, zb, sgb, nar, sa, sq = _front(xp, front_consts, tm=256, shift=1, n_seq=bp)
    y_prompt, s_new = _delta(q, k, v, nar, n_seq=bp, chunk=PROMPT_CHUNK, cpb=4,
                             back=(xp, zb, ma, sgb, back_consts))
    y_prompt = y_prompt.reshape(bp, tp, D_MODEL)
    new_conv_a_prompt = sa[None]
    new_conv_qkv_prompt = sq[None]
    new_delta_prompt = s_new[None]

    xs = jnp.transpose(x_sample, (1, 0, 2)).reshape(ts * bs, D_MODEL)
    sta = jnp.transpose(state_conv_a[0], (1, 0, 2)).reshape((CONV_A_W - 1) * bs, D_MODEL)
    stq = jnp.transpose(state_conv_qkv[0], (1, 0, 2)).reshape((CONV_B_W - 1) * bs, QKV_W)
    ma, q, k, v, zb, sgb, nar, cha, pqkv = _front(
        xs, front_consts, tm=bs, shift=bs, n_seq=1, state=(sta, stq))
    sa = jnp.concatenate([sta, cha], axis=0)[ts * bs:]
    sq = jnp.concatenate([stq, pqkv], axis=0)[ts * bs:]
    tpad = 8

    def to_batch_major(a):
        a = jnp.transpose(a.reshape(ts, bs, a.shape[-1]), (1, 0, 2)).astype(F32)
        return jnp.pad(a, ((0, 0), (0, tpad - ts), (0, 0))).reshape(bs * tpad, a.shape[-1])

    o, s_new = _delta(to_batch_major(q), to_batch_major(k), to_batch_major(v),
                      to_batch_major(nar), n_seq=bs, chunk=tpad, cpb=1, spb=8, s0=state_delta[0])
    o = jnp.transpose(o.reshape(bs, tpad, D_MODEL)[:, :ts], (1, 0, 2)).reshape(ts * bs, D_MODEL)
    ys = _back(xs, o, zb, ma, sgb, back_consts, tm=ts * bs)
    y_sample = jnp.transpose(ys.reshape(ts, bs, D_MODEL), (1, 0, 2))
    new_conv_a_sample = jnp.transpose(sa.reshape(CONV_A_W - 1, bs, D_MODEL), (1, 0, 2))[None]
    new_conv_qkv_sample = jnp.transpose(sq.reshape(CONV_B_W - 1, bs, QKV_W), (1, 0, 2))[None]
    new_delta_sample = s_new[None]

    return (y_prompt, y_sample, new_conv_a_prompt, new_conv_qkv_prompt, new_delta_prompt,
            new_conv_a_sample, new_conv_qkv_sample, new_delta_sample)
```

```python
import functools
import math

import jax
import jax.numpy as jnp
from jax import lax
from jax.experimental import pallas as pl
from jax.experimental.pallas import tpu as pltpu

D_MODEL = 1024
N_HEADS = 8
HEAD_DIM = 128
QKV_W = 3 * D_MODEL
CONV_A_W = 3
CONV_B_W = 4
PROMPT_CHUNK = 64
EPS = 1e-6
NARROW = 128
OFF_B, OFF_C, OFF_H, OFF_Z, OFF_QKV, OFF_ZB, OFF_GA, OFF_GB, OFF_NARROW = (
    0, 1024, 2048, 3072, 4096, 7168, 8192, 9216, 10240)

V7X_VMEM_LIMIT = 56 * 1024 * 1024
COL_CHUNK = 256

F32 = jnp.float32
BF16 = jnp.bfloat16


def _dot(a, b):
    return jnp.dot(a, b, preferred_element_type=F32)


def _silu(x):
    return x * jax.nn.sigmoid(x)


GROUP = PROMPT_CHUNK


def _group_time(r):
    return ((r & 7) << 3) | ((r >> 3) & 7)


def _group_perm(n):
    r = lax.broadcasted_iota(jnp.int32, (n, n), 0)
    c = lax.broadcasted_iota(jnp.int32, (n, n), 1)
    return (c == ((r & ~(GROUP - 1)) | _group_time(r))).astype(BF16)


def _softplus(x):
    return jnp.maximum(x, 0.0) + jnp.log(1.0 + jnp.exp(-jnp.abs(x)))


def _front_kernel(tm, shift, tiles_per_seq, has_state, *refs):
    if has_state:
        (x_ref, nw_ref, wm_ref, wt_ref, caw_ref, cbw_ref, hp_ref, woa_ref, sta_ref, stq_ref,
         ma_ref, q_ref, k_ref, v_ref, zb_ref, sgb_ref, nar_ref, sa_ref, sq_ref,
         xn_s, exta_s, extb_s, pre_s, *stage_s) = refs
    else:
        (x_ref, nw_ref, wm_ref, wt_ref, caw_ref, cbw_ref, hp_ref, woa_ref,
         ma_ref, q_ref, k_ref, v_ref, zb_ref, sgb_ref, nar_ref, sa_ref, sq_ref,
         xn_s, exta_s, extb_s, pre_s, *stage_s) = refs
        sta_ref = stq_ref = None
    grouped = shift == 1
    na = (CONV_A_W - 1) * shift
    nb = (CONV_B_W - 1) * shift
    first = (pl.program_id(0) % tiles_per_seq) == 0

    x = x_ref[...]
    var = jnp.mean(x * x, axis=-1, keepdims=True)
    xn = (x * lax.rsqrt(var + EPS) * nw_ref[...]).astype(BF16)
    xn_s[...] = _dot(_group_perm(tm), xn).astype(BF16) if grouped else xn

    def proj(off, c0, width=COL_CHUNK):
        w_ref, lo = (wm_ref, off + c0) if off < OFF_GA else (wt_ref, off - OFF_GA + c0)
        return _dot(xn_s[...], w_ref[:, lo:lo + width])

    if grouped:
        assert not has_state

        @pl.when(pl.program_id(0) == 0)
        def _():
            exta_s[...] = jnp.zeros(exta_s.shape, F32)
            extb_s[...] = jnp.zeros(extb_s.shape, F32)

        for ext_s in (exta_s, extb_s):
            ext_s[...] = jnp.where(first, 0.0, ext_s[...])

        def causal_conv(ext_s, width, w_ref, cur, c0, c1):
            taps = width - 1
            sub0 = lax.broadcasted_iota(jnp.int32, (8, c1 - c0), 0) == 0
            prev_tail = [ext_s[8 * i:8 * i + 8, c0:c1] for i in range(taps)]
            shifted = [[] for _ in range(taps)]
            for g0 in range(0, tm, GROUP):
                vrow = lambda v: cur[g0 + 8 * v:g0 + 8 * v + 8]
                tail = [vrow(8 - taps + i) for i in range(taps)]
                wrap = [jnp.where(sub0, pltpu.roll(p, 1, 0), pltpu.roll(t, 1, 0))
                        for p, t in zip(prev_tail, tail)]
                for d in range(1, taps + 1):
                    shifted[d - 1] += [vrow(v - d) if v >= d else wrap[v - d + taps]
                                       for v in range(8)]
                prev_tail = tail
            for i in range(taps):
                ext_s[8 * i:8 * i + 8, c0:c1] = prev_tail[i]
            conv = cur * w_ref[width - 1:width, c0:c1]
            for d in range(1, taps + 1):
                conv = conv + (jnp.concatenate(shifted[d - 1], axis=0)
                               * w_ref[width - 1 - d:width - d, c0:c1])
            return conv

        def last_rows(cur, n):
            rows = [tm - GROUP + _group_time(t) for t in range(GROUP - n, GROUP)]
            return [cur[r:r + 1] for r in rows]
    else:
        ha = exta_s.shape[0] - tm
        hb = extb_s.shape[0] - tm

        @pl.when(pl.program_id(0) == 0)
        def _():
            exta_s[tm:tm + ha, :] = jnp.zeros((ha, D_MODEL), F32)
            extb_s[tm:tm + hb, :] = jnp.zeros((hb, QKV_W), F32)

        for ext_s, st_ref, hrows in ((exta_s, sta_ref, ha), (extb_s, stq_ref, hb)):
            for c0 in range(0, ext_s.shape[1], COL_CHUNK):
                cs = slice(c0, c0 + COL_CHUNK)
                init = st_ref[:, cs] if has_state else jnp.zeros((hrows, COL_CHUNK), F32)
                ext_s[0:hrows, cs] = jnp.where(first, init, ext_s[tm:tm + hrows, cs])

        def causal_conv(ext_s, width, w_ref, cur, c0, c1):
            hrows = ext_s.shape[0] - tm
            ext_s[hrows:hrows + tm, c0:c1] = cur
            conv = cur * w_ref[width - 1:width, c0:c1]
            for j in range(width - 1):
                d = (width - 1 - j) * shift
                conv = conv + ext_s[hrows - d:hrows - d + tm, c0:c1] * w_ref[j:j + 1, c0:c1]
            return conv

    def narrow_dots():
        return (proj(OFF_NARROW, 0, NARROW),)

    def narrow_epi(nar):
        lane = lax.broadcasted_iota(jnp.int32, nar.shape, 1)
        beta = jax.nn.sigmoid(nar)
        gdec = -jnp.exp(hp_ref[0:1, :]) * _softplus(nar + hp_ref[1:2, :])
        nar_ref[...] = jnp.where(lane < N_HEADS, beta, jnp.where(lane < 2 * N_HEADS, gdec, 0.0))

    def branch_a_dots(c0):
        return tuple(proj(off, c0) for off in (OFF_B, OFF_C, OFF_H, OFF_Z))

    def branch_a_epi(c0, pb, pc, ph, pz):
        c1 = c0 + COL_CHUNK
        ch = pc * ph
        if has_state:
            sa_ref[:, c0:c1] = ch
        elif grouped:
            for i, r in enumerate(last_rows(ch, na)):
                sa_ref[i:i + 1, c0:c1] = r
        conv = causal_conv(exta_s, CONV_A_W, caw_ref, ch, c0, c1)
        pre_s[:, c0:c1] = (_silu(pz) * pb * conv).astype(BF16)

    def gate_a_dots(c0):
        return (_dot(pre_s[...], woa_ref[:, c0:c0 + COL_CHUNK]), proj(OFF_GA, c0))

    def gate_a_epi(c0, ya, ga):
        ma_ref[:, c0:c0 + COL_CHUNK] = (jax.nn.sigmoid(ga) * ya).astype(ma_ref.dtype)

    def qkv_dots(e0):
        return (proj(OFF_QKV, e0),)

    def qkv_epi(e0, p):
        g, c0 = divmod(e0, D_MODEL)
        out_ref = (q_ref, k_ref, v_ref)[g]
        if has_state:
            sq_ref[:, e0:e0 + COL_CHUNK] = p
        elif grouped:
            for i, r in enumerate(last_rows(p, nb)):
                sq_ref[i:i + 1, e0:e0 + COL_CHUNK] = r
        s = _silu(causal_conv(extb_s, CONV_B_W, cbw_ref, p, e0, e0 + COL_CHUNK))
        if g == 2:
            out_ref[:, c0:c0 + COL_CHUNK] = s.astype(out_ref.dtype)
            return
        scale = HEAD_DIM ** -0.5 if g == 0 else 1.0
        for h0 in range(0, COL_CHUNK, HEAD_DIM):
            sh = s[:, h0:h0 + HEAD_DIM]
            ss = jnp.sum(sh * sh, axis=-1, keepdims=True)
            out_ref[:, c0 + h0:c0 + h0 + HEAD_DIM] = (
                sh * (lax.rsqrt(ss + EPS) * scale)).astype(out_ref.dtype)

    def act_dots(off, c0):
        return (proj(off, c0),)

    def zb_epi(c0, p):
        zb_ref[:, c0:c0 + COL_CHUNK] = _silu(p).astype(zb_ref.dtype)

    def sgb_epi(c0, p):
        sgb_ref[:, c0:c0 + COL_CHUNK] = jax.nn.sigmoid(p).astype(sgb_ref.dtype)

    part = functools.partial
    cols = list(range(0, D_MODEL, COL_CHUNK))
    n_col = len(cols)
    stages = [[(part(qkv_dots, e0), part(qkv_epi, e0))] for e0 in range(0, QKV_W, COL_CHUNK)]
    for n, c0 in enumerate(cols):
        stages[2 * n].append((part(branch_a_dots, c0), part(branch_a_epi, c0)))
        stages[2 * n + 1] += [(part(act_dots, OFF_ZB, c0), part(zb_epi, c0)),
                              (part(act_dots, OFF_GB, c0), part(sgb_epi, c0))]
        stages[2 * n_col + n].append((part(gate_a_dots, c0), part(gate_a_epi, c0)))
    stages[2 * n_col].append((narrow_dots, narrow_epi))
    assert len(stages) == 3 * n_col

    def run_dots(i):
        widths, k = [], 0
        for dots, _ in stages[i]:
            outs = dots()
            for r in outs:
                stage_s[i % 2][k, :, 0:r.shape[1]] = r
                k += 1
            widths.append([r.shape[1] for r in outs])
        return widths

    widths = run_dots(0)
    for i, stage in enumerate(stages):
        ready = widths
        if i + 1 < len(stages):
            widths = run_dots(i + 1)
        k = 0
        for (_, epilogue), ws in zip(stage, ready):
            epilogue(*[stage_s[i % 2][k + n, :, 0:w] for n, w in enumerate(ws)])
            k += len(ws)
    if not has_state and not grouped:
        sa_ref[...] = exta_s[ha + tm - na:ha + tm, :]
        sq_ref[...] = extb_s[hb + tm - nb:hb + tm, :]


def _resident(shape):
    nd = len(shape)
    return pl.BlockSpec(shape, lambda *_: (0,) * nd, pipeline_mode=pl.Buffered(1))


def _front(x2d, consts, *, tm, shift, n_seq, state=None):
    rows = x2d.shape[0]
    n_tiles = rows // tm
    tiles_per_seq = n_tiles // n_seq
    na = (CONV_A_W - 1) * shift
    nb = (CONV_B_W - 1) * shift
    hdr_a = -(-na // 8) * 8
    hdr_b = -(-nb // 8) * 8
    has_state = state is not None
    nw, wm, wt, caw, cbw, hp, woa = consts

    row_spec = lambda w: pl.BlockSpec((tm, w), lambda i: (i, 0))
    in_specs = [row_spec(D_MODEL), _resident(nw.shape), _resident(wm.shape), _resident(wt.shape),
                _resident(caw.shape), _resident(cbw.shape), _resident(hp.shape),
                _resident(woa.shape)]
    args = [x2d, nw, wm, wt, caw, cbw, hp, woa]
    out_specs = [row_spec(D_MODEL)] * 6 + [row_spec(NARROW)]
    out_shape = [jax.ShapeDtypeStruct((rows, D_MODEL), BF16)] * 6 + [
        jax.ShapeDtypeStruct((rows, NARROW), F32)]
    if has_state:
        sta, stq = state
        assert sta.shape[0] == hdr_a == na and stq.shape[0] == hdr_b == nb
        in_specs += [_resident(sta.shape), _resident(stq.shape)]
        args += [sta, stq]
        out_specs += [row_spec(D_MODEL), row_spec(QKV_W)]
        out_shape += [jax.ShapeDtypeStruct((rows, D_MODEL), F32),
                      jax.ShapeDtypeStruct((rows, QKV_W), F32)]
    else:
        seq_spec = lambda r, w: pl.BlockSpec((None, r, w), lambda i: (i // tiles_per_seq, 0, 0))
        out_specs += [seq_spec(na, D_MODEL), seq_spec(nb, QKV_W)]
        out_shape += [jax.ShapeDtypeStruct((n_seq, na, D_MODEL), F32),
                      jax.ShapeDtypeStruct((n_seq, nb, QKV_W), F32)]
    if shift == 1:
        assert tm % GROUP == 0
        ext_rows = (8 * (CONV_A_W - 1), 8 * (CONV_B_W - 1))
    else:
        ext_rows = (hdr_a + tm, hdr_b + tm)
    scratch = [pltpu.VMEM((tm, D_MODEL), BF16),
               pltpu.VMEM((ext_rows[0], D_MODEL), F32),
               pltpu.VMEM((ext_rows[1], QKV_W), F32),
               pltpu.VMEM((tm, D_MODEL), BF16),
               pltpu.VMEM((5, tm, COL_CHUNK), F32), pltpu.VMEM((5, tm, COL_CHUNK), F32)]
    return pl.pallas_call(
        functools.partial(_front_kernel, tm, shift, tiles_per_seq, has_state),
        grid=(n_tiles,),
        in_specs=in_specs,
        out_specs=out_specs,
        out_shape=out_shape,
        scratch_shapes=scratch,
        compiler_params=pltpu.CompilerParams(
            dimension_semantics=("arbitrary",), vmem_limit_bytes=V7X_VMEM_LIMIT),
        name="front",
    )(*args)


def _delta_kernel(chunk, cpb, spb, n_blocks, blocks_per_seq, has_init, fused_back, grouped,
                  *refs):
    refs = list(refs)
    q_ref, k_ref, v_ref, nar_ref = refs[:4]
    del refs[:4]
    s0_ref = refs.pop(0) if has_init else None
    if fused_back:
        back_in = refs[:8]
        del refs[:8]
    o_ref, so_ref, st_s, up_s, wdq_s, kdt_s, aqk_s, dl_s, tp_s = refs[:9]
    C = chunk
    nch = spb * cpb
    levels = int(math.log2(C))
    grp = min(N_HEADS, 256 // C)
    W = grp * C
    sdt = wdq_s.dtype
    mx = lambda a: a.astype(BF16)
    step = pl.program_id(0)
    block = jnp.minimum(step, n_blocks - 1)
    fresh = (block % blocks_per_seq) == 0

    if fused_back:
        o_s, on_s, m_s = refs[9:]
        y_ref, o_dst = o_ref, o_s

        @pl.when(step == 0)
        def _():
            st_s[...] = jnp.zeros(st_s.shape, F32)
            o_s[...] = jnp.zeros(o_s.shape, F32)

        _back_kernel(back_in[0], o_s, *back_in[1:], y_ref, on_s, m_s, grouped=grouped)
    else:
        o_dst = o_ref

        @pl.when(step == 0)
        def _():
            st_s[...] = jnp.zeros(st_s.shape, F32)

    row = lax.broadcasted_iota(jnp.int32, (C, C), 0)
    col = lax.broadcasted_iota(jnp.int32, (C, C), 1)
    row_t, col_t = (_group_time(row), _group_time(col)) if grouped else (row, col)
    causal = row_t >= col_t
    strict = row_t > col_t
    ltri = causal.astype(BF16)
    utri = (row_t <= col_t).astype(BF16)
    rw = lax.broadcasted_iota(jnp.int32, (W, W), 0)
    cw = lax.broadcasted_iota(jnp.int32, (W, W), 1)
    blockdiag = (rw >> levels) == (cw >> levels)
    ri = lax.broadcasted_iota(jnp.int32, (C, W), 0)
    ci = lax.broadcasted_iota(jnp.int32, (C, W), 1)
    eye_cat = (ri == (ci & (C - 1))).astype(F32)
    groups = [(c, g) for c in range(nch) for g in range(N_HEADS // grp)]

    def cumsum_t(a, tri, dims):
        hi = a.astype(BF16)
        lo = (a - hi.astype(F32)).astype(BF16)
        dn = (dims, ((), ()))
        if dims[0] == (1,):
            return (lax.dot_general(tri, hi, dn, preferred_element_type=F32)
                    + lax.dot_general(tri, lo, dn, preferred_element_type=F32))
        return (lax.dot_general(hi, tri, dn, preferred_element_type=F32)
                + lax.dot_general(lo, tri, dn, preferred_element_type=F32))

    heads = [(c, h) for c in range(nch) for h in range(N_HEADS)]
    hsl = lambda h: slice(h * HEAD_DIM, (h + 1) * HEAD_DIM)
    rsl = lambda c: slice(c * C, (c + 1) * C)
    ident = (row == col).astype(BF16)
    gcs = [cumsum_t(nar_ref[rsl(c), :], ltri, ((1,), (0,))) for c in range(nch)]
    gcts = [cumsum_t(nar_ref[rsl(c), :], utri, ((0,), (0,))) for c in range(nch)]
    narts = [cumsum_t(nar_ref[rsl(c), :], ident, ((0,), (0,))) for c in range(nch)]
    a_kq = {}
    for c, h in heads:
        kx = mx(k_ref[rsl(c), hsl(h)])
        a_kq[c, h] = lax.dot_general(
            jnp.concatenate([kx, mx(q_ref[rsl(c), hsl(h)])], axis=0), kx,
            (((1,), (1,)), ((), ())), preferred_element_type=F32)
    for c in range(nch):
        dl_s[c] = jnp.exp(gcs[c][C - 1:C, :])
    for c, h in heads:
        g, j = divmod(h, grp)
        gcw = jnp.broadcast_to(gcs[c][:, N_HEADS + h:N_HEADS + h + 1], (C, HEAD_DIM))
        betw = jnp.broadcast_to(nar_ref[rsl(c), h:h + 1], (C, C))
        gcr = gcts[c][N_HEADS + h:N_HEADS + h + 1, :]
        decay = jnp.exp(jnp.where(causal, gcw[:, 0:C] - gcr, -jnp.inf))
        wdq_s[c, h, C:2 * C, :] = (q_ref[rsl(c), hsl(h)].astype(F32) * jnp.exp(gcw)).astype(sdt)
        k_t = k_ref[rsl(c), hsl(h)].astype(F32).T
        kdt_s[c, h] = (k_t * jnp.exp(gcr[:, C - 1:C] - gcr)).astype(sdt)
        a = a_kq[c, h]
        tp_s[c, g, C:2 * C, j * C:(j + 1) * C] = -jnp.where(strict, a[:C] * betw * decay, 0.0)
        aqk_s[c, h] = jnp.where(causal, a[C:] * decay, 0.0).astype(sdt)

    for c, g in groups:
        tp_s[c, g, 0:C, :] = eye_cat + tp_s[c, g, C:2 * C, :]
    for lv in range(levels):
        for c, g in groups:
            p = tp_s[c, g, C:2 * C, :]
            bd = mx(jnp.where(blockdiag, jnp.concatenate([p] * grp, axis=0), 0.0))
            if lv == 0:
                tp_s[c, g, C:2 * C, :] = _dot(mx(p), bd)
            elif lv < levels - 1:
                xx = _dot(mx(tp_s[c, g]), bd)
                tp_s[c, g, 0:C, :] = tp_s[c, g, 0:C, :] + xx[:C]
                tp_s[c, g, C:2 * C, :] = xx[C:]
            else:
                t = tp_s[c, g, 0:C, :]
                tp_s[c, g, 0:C, :] = t + _dot(mx(t), bd)

    for c, h in heads:
        g, j = divmod(h, grp)
        tb = tp_s[c, g, 0:C, j * C:(j + 1) * C] * narts[c][h:h + 1, :]
        tbe = tb * jnp.exp(gcts[c][N_HEADS + h:N_HEADS + h + 1, :])
        up_s[c, h] = _dot(mx(tb), mx(v_ref[rsl(c), hsl(h)]))
        wdq_s[c, h, 0:C, :] = _dot(mx(tbe), mx(k_ref[rsl(c), hsl(h)])).astype(sdt)

    init = s0_ref[...] if has_init else jnp.zeros(st_s.shape, F32)
    st_s[...] = jnp.where(fresh, init, st_s[...])
    for cc in range(cpb):
        chains = [(s * cpb + cc, s, h) for s in range(spb) for h in range(N_HEADS)]
        rs = [_dot(mx(wdq_s[c, h]), mx(st_s[s, h])) for c, s, h in chains]
        us = [mx(up_s[c, h] - r[:C]) for (c, s, h), r in zip(chains, rs)]
        for (c, s, h), r, u in zip(chains, rs, us):
            o = r[C:] + _dot(mx(aqk_s[c, h]), u)
            o_dst[c * C:(c + 1) * C, h * HEAD_DIM:(h + 1) * HEAD_DIM] = o.astype(o_dst.dtype)
        for (c, s, h), u in zip(chains, us):
            dl = dl_s[c][:, N_HEADS + h:N_HEADS + h + 1]
            new = st_s[s, h] * dl + _dot(mx(kdt_s[c, h]), u)
            st_s[s, h] = jnp.where(step < n_blocks, new, st_s[s, h]) if fused_back else new

    so_ref[...] = st_s[...]


def _delta(q, k, v, nar, *, n_seq, chunk, cpb, spb=1, s0=None, back=None, grouped=False):
    rows = q.shape[0]
    seq_len = rows // n_seq
    tb = chunk * cpb
    blocks_per_seq = seq_len // tb
    assert spb == 1 or blocks_per_seq == 1
    n_blocks = (n_seq // spb) * blocks_per_seq
    has_init = s0 is not None
    fused_back = back is not None
    nch = spb * cpb
    cur = lambda s: jnp.minimum(s, n_blocks - 1)
    prev = lambda s: jnp.maximum(s - 1, 0)
    row_spec = lambda w, at: pl.BlockSpec((spb * tb, w), lambda s: (at(s), 0))
    st_shape = (spb, N_HEADS, HEAD_DIM, HEAD_DIM)
    in_specs = [row_spec(D_MODEL, cur)] * 3 + [row_spec(NARROW, cur)]
    args = [q, k, v, nar]
    if has_init:
        in_specs.append(pl.BlockSpec(st_shape, lambda s: (cur(s), 0, 0, 0)))
        args.append(s0)
    sdt = BF16 if chunk % 16 == 0 else F32
    grp = min(N_HEADS, 256 // chunk)
    scratch = [pltpu.VMEM(st_shape, F32),
               pltpu.VMEM((nch, N_HEADS, chunk, HEAD_DIM), F32),
               pltpu.VMEM((nch, N_HEADS, 2 * chunk, HEAD_DIM), sdt),
               pltpu.VMEM((nch, N_HEADS, HEAD_DIM, chunk), sdt),
               pltpu.VMEM((nch, N_HEADS, chunk, chunk), sdt),
               pltpu.VMEM((nch, 1, NARROW), F32),
               pltpu.VMEM((nch, N_HEADS // grp, 2 * chunk, grp * chunk), F32)]
    if fused_back:
        x2d, zb, ma, sgb, consts = back
        in_specs += [row_spec(D_MODEL, prev)] * 4 + [_resident(c.shape) for c in consts]
        args += [x2d, zb, ma, sgb, *consts]
        out0 = (row_spec(D_MODEL, prev), jax.ShapeDtypeStruct((rows, D_MODEL), F32))
        scratch += [pltpu.VMEM((spb * tb, D_MODEL), F32),
                    pltpu.VMEM((spb * tb, D_MODEL), BF16), pltpu.VMEM((spb * tb, D_MODEL), BF16)]
    else:
        out0 = (row_spec(D_MODEL, cur), jax.ShapeDtypeStruct((rows, D_MODEL), q.dtype))
    return pl.pallas_call(
        functools.partial(_delta_kernel, chunk, cpb, spb, n_blocks, blocks_per_seq, has_init,
                          fused_back, grouped),
        grid=(n_blocks + 1 if fused_back else n_blocks,),
        in_specs=in_specs,
        out_specs=[out0[0],
                   pl.BlockSpec(st_shape, lambda s: (cur(s) // blocks_per_seq, 0, 0, 0))],
        out_shape=[out0[1], jax.ShapeDtypeStruct((n_seq, N_HEADS, HEAD_DIM, HEAD_DIM), F32)],
        scratch_shapes=scratch,
        compiler_params=pltpu.CompilerParams(
            dimension_semantics=("arbitrary",), vmem_limit_bytes=V7X_VMEM_LIMIT),
        name="delta",
    )(*args)


def _back_kernel(x_ref, o_ref, zb_ref, ma_ref, sgb_ref, onw_ref, wob_ref, wo_ref, fw_ref,
                 y_ref, on_s, m_s, grouped=False):
    for h in range(N_HEADS):
        hs = slice(h * HEAD_DIM, (h + 1) * HEAD_DIM)
        oh = o_ref[:, hs].astype(F32)
        ms = jnp.mean(oh * oh, axis=-1, keepdims=True)
        on = oh * lax.rsqrt(ms + EPS) * onw_ref[...]
        on_s[:, hs] = (on * zb_ref[:, hs].astype(F32)).astype(BF16)
    for c0 in range(0, D_MODEL, COL_CHUNK):
        c1 = c0 + COL_CHUNK
        yb = _dot(on_s[...], wob_ref[:, c0:c1])
        m = ma_ref[:, c0:c1].astype(F32) + sgb_ref[:, c0:c1].astype(F32) * yb
        m_s[:, c0:c1] = m.astype(BF16)
    m = m_s[...]
    if grouped:
        m = _dot(_group_perm(m.shape[0]), m).astype(BF16)
    hres = x_ref[...] + _dot(m, wo_ref[...])
    var = jnp.mean(hres * hres, axis=-1, keepdims=True)
    y_ref[...] = hres * lax.rsqrt(var + EPS) * fw_ref[...]


def _back(x2d, o, zb, ma, sgb, consts, *, tm):
    rows = x2d.shape[0]
    onw, wob, wo, fw = consts
    row_spec = pl.BlockSpec((tm, D_MODEL), lambda i: (i, 0))
    return pl.pallas_call(
        _back_kernel,
        grid=(rows // tm,),
        in_specs=[row_spec] * 5 + [_resident(onw.shape), _resident(wob.shape),
                                   _resident(wo.shape), _resident(fw.shape)],
        out_specs=row_spec,
        out_shape=jax.ShapeDtypeStruct((rows, D_MODEL), F32),
        scratch_shapes=[pltpu.VMEM((tm, D_MODEL), BF16), pltpu.VMEM((tm, D_MODEL), BF16)],
        compiler_params=pltpu.CompilerParams(
            dimension_semantics=("arbitrary",), vmem_limit_bytes=V7X_VMEM_LIMIT),
        name="back",
    )(x2d, o, zb, ma, sgb, onw, wob, wo, fw)


def kernel(x_prompt, x_sample, state_conv_a, state_conv_qkv, state_delta, w_in, conv_a_w,
           conv_b_w, a_log, dt_bias, onorm_w, w_out_a, w_out_b, w_o, norm_w, final_norm_w):
    assert w_in.shape[0] == 1, "single layer"
    bp, tp, _ = x_prompt.shape
    bs, ts, _ = x_sample.shape
    wm = w_in[0].astype(BF16)
    n_wide = OFF_ZB + D_MODEL
    wn = jnp.pad(wm[:, n_wide:n_wide + 2 * N_HEADS], ((0, 0), (0, NARROW - 2 * N_HEADS)))
    wt = jnp.concatenate([wm[:, n_wide + 2 * N_HEADS:], wn], axis=1)
    hp = jnp.zeros((8, NARROW), F32)
    hp = hp.at[0, N_HEADS:2 * N_HEADS].set(a_log[0]).at[1, N_HEADS:2 * N_HEADS].set(dt_bias[0])
    front_consts = (norm_w[0][None, :], wm, wt, conv_a_w[0], conv_b_w[0], hp,
                    w_out_a[0].astype(BF16))
    back_consts = (onorm_w[0][None, :], w_out_b[0].astype(BF16), w_o[0].astype(BF16),
                   final_norm_w[None, :])

    xp = x_prompt.reshape(bp * tp, D_MODEL)
    ma, q, k, v, zb, sgb, nar, sa, sq = _front(xp, front_consts, tm=256, shift=1, n_seq=bp)
    y_prompt, s_new = _delta(q, k, v, nar, n_seq=bp, chunk=PROMPT_CHUNK, cpb=4, grouped=True,
                             back=(xp, zb, ma, sgb, back_consts))
    y_prompt = y_prompt.reshape(bp, tp, D_MODEL)
    new_conv_a_prompt = sa[None]
    new_conv_qkv_prompt = sq[None]
    new_delta_prompt = s_new[None]

    xs = jnp.transpose(x_sample, (1, 0, 2)).reshape(ts * bs, D_MODEL)
    sta = jnp.transpose(state_conv_a[0], (1, 0, 2)).reshape((CONV_A_W - 1) * bs, D_MODEL)
    stq = jnp.transpose(state_conv_qkv[0], (1, 0, 2)).reshape((CONV_B_W - 1) * bs, QKV_W)
    ma, q, k, v, zb, sgb, nar, cha, pqkv = _front(
        xs, front_consts, tm=bs, shift=bs, n_seq=1, state=(sta, stq))
    sa = jnp.concatenate([sta, cha], axis=0)[ts * bs:]
    sq = jnp.concatenate([stq, pqkv], axis=0)[ts * bs:]
    tpad = 8

    def to_batch_major(a):
        a = jnp.transpose(a.reshape(ts, bs, a.shape[-1]), (1, 0, 2)).astype(F32)
        return jnp.pad(a, ((0, 0), (0, tpad - ts), (0, 0))).reshape(bs * tpad, a.shape[-1])

    o, s_new = _delta(to_batch_major(q), to_batch_major(k), to_batch_major(v),
                      to_batch_major(nar), n_seq=bs, chunk=tpad, cpb=1, spb=8, s0=state_delta[0])
    o = jnp.transpose(o.reshape(bs, tpad, D_MODEL)[:, :ts], (1, 0, 2)).reshape(ts * bs, D_MODEL)
    ys = _back(xs, o, zb, ma, sgb, back_consts, tm=ts * bs)
    y_sample = jnp.transpose(ys.reshape(ts, bs, D_MODEL), (1, 0, 2))
    new_conv_a_sample = jnp.transpose(sa.reshape(CONV_A_W - 1, bs, D_MODEL), (1, 0, 2))[None]
    new_conv_qkv_sample = jnp.transpose(sq.reshape(CONV_B_W - 1, bs, QKV_W), (1, 0, 2))[None]
    new_delta_sample = s_new[None]

    return (y_prompt, y_sample, new_conv_a_prompt, new_conv_qkv_prompt, new_delta_prompt,
            new_conv_a_sample, new_conv_qkv_sample, new_delta_sample)
```

```python
import functools
import math

import jax
import jax.numpy as jnp
from jax import lax
from jax.experimental import pallas as pl
from jax.experimental.pallas import tpu as pltpu

D_MODEL = 1024
N_HEADS = 8
HEAD_DIM = 128
QKV_W = 3 * D_MODEL
CONV_A_W = 3
CONV_B_W = 4
PROMPT_CHUNK = 64
EPS = 1e-6
NARROW = 128
OFF_B, OFF_C, OFF_H, OFF_Z, OFF_QKV, OFF_ZB, OFF_GA, OFF_GB, OFF_NARROW = (
    0, 1024, 2048, 3072, 4096, 7168, 8192, 9216, 10240)

V7X_VMEM_LIMIT = 56 * 1024 * 1024
COL_CHUNK = 256

F32 = jnp.float32
BF16 = jnp.bfloat16


def _dot(a, b):
    return jnp.dot(a, b, preferred_element_type=F32)


def _silu(x):
    return x * jax.nn.sigmoid(x)


GROUP = PROMPT_CHUNK


def _group_time(r):
    return ((r & 7) << 3) | ((r >> 3) & 7)


def _group_perm(n):
    r = lax.broadcasted_iota(jnp.int32, (n, n), 0)
    c = lax.broadcasted_iota(jnp.int32, (n, n), 1)
    return (c == ((r & ~(GROUP - 1)) | _group_time(r))).astype(BF16)


def _softplus(x):
    return jnp.maximum(x, 0.0) + jnp.log(1.0 + jnp.exp(-jnp.abs(x)))


def _front_kernel(tm, shift, tiles_per_seq, has_state, *refs):
    if has_state:
        (x_ref, nw_ref, wm_ref, wt_ref, caw_ref, cbw_ref, hp_ref, woa_ref, sta_ref, stq_ref,
         ma_ref, q_ref, k_ref, v_ref, zb_ref, sgb_ref, nar_ref, sa_ref, sq_ref,
         xn_s, exta_s, extb_s, pre_s, *stage_s) = refs
    else:
        (x_ref, nw_ref, wm_ref, wt_ref, caw_ref, cbw_ref, hp_ref, woa_ref,
         ma_ref, q_ref, k_ref, v_ref, zb_ref, sgb_ref, nar_ref, sa_ref, sq_ref,
         xn_s, exta_s, extb_s, pre_s, *stage_s) = refs
        sta_ref = stq_ref = None
    grouped = shift == 1
    na = (CONV_A_W - 1) * shift
    nb = (CONV_B_W - 1) * shift
    first = (pl.program_id(0) % tiles_per_seq) == 0

    x = x_ref[...]
    var = jnp.mean(x * x, axis=-1, keepdims=True)
    xn = (x * lax.rsqrt(var + EPS) * nw_ref[...]).astype(BF16)
    xn_s[...] = _dot(_group_perm(tm), xn).astype(BF16) if grouped else xn

    def proj(off, c0, width=COL_CHUNK):
        w_ref, lo = (wm_ref, off + c0) if off < OFF_GA else (wt_ref, off - OFF_GA + c0)
        return _dot(xn_s[...], w_ref[:, lo:lo + width])

    if grouped:
        assert not has_state

        @pl.when(pl.program_id(0) == 0)
        def _():
            exta_s[...] = jnp.zeros(exta_s.shape, F32)
            extb_s[...] = jnp.zeros(extb_s.shape, F32)

        for ext_s in (exta_s, extb_s):
            ext_s[...] = jnp.where(first, 0.0, ext_s[...])

        def causal_conv(ext_s, width, w_ref, cur, c0, c1):
            taps = width - 1
            sub0 = lax.broadcasted_iota(jnp.int32, (8, c1 - c0), 0) == 0
            prev_tail = [ext_s[8 * i:8 * i + 8, c0:c1] for i in range(taps)]
            shifted = [[] for _ in range(taps)]
            for g0 in range(0, tm, GROUP):
                vrow = lambda v: cur[g0 + 8 * v:g0 + 8 * v + 8]
                tail = [vrow(8 - taps + i) for i in range(taps)]
                wrap = [jnp.where(sub0, pltpu.roll(p, 1, 0), pltpu.roll(t, 1, 0))
                        for p, t in zip(prev_tail, tail)]
                for d in range(1, taps + 1):
                    shifted[d - 1] += [vrow(v - d) if v >= d else wrap[v - d + taps]
                                       for v in range(8)]
                prev_tail = tail
            for i in range(taps):
                ext_s[8 * i:8 * i + 8, c0:c1] = prev_tail[i]
            conv = cur * w_ref[width - 1:width, c0:c1]
            for d in range(1, taps + 1):
                conv = conv + (jnp.concatenate(shifted[d - 1], axis=0)
                               * w_ref[width - 1 - d:width - d, c0:c1])
            return conv

        def last_rows(cur, n):
            rows = [tm - GROUP + _group_time(t) for t in range(GROUP - n, GROUP)]
            return [cur[r:r + 1] for r in rows]
    else:
        ha = exta_s.shape[0] - tm
        hb = extb_s.shape[0] - tm

        @pl.when(pl.program_id(0) == 0)
        def _():
            exta_s[tm:tm + ha, :] = jnp.zeros((ha, D_MODEL), F32)
            extb_s[tm:tm + hb, :] = jnp.zeros((hb, QKV_W), F32)

        for ext_s, st_ref, hrows in ((exta_s, sta_ref, ha), (extb_s, stq_ref, hb)):
            for c0 in range(0, ext_s.shape[1], COL_CHUNK):
                cs = slice(c0, c0 + COL_CHUNK)
                init = st_ref[:, cs] if has_state else jnp.zeros((hrows, COL_CHUNK), F32)
                ext_s[0:hrows, cs] = jnp.where(first, init, ext_s[tm:tm + hrows, cs])

        def causal_conv(ext_s, width, w_ref, cur, c0, c1):
            hrows = ext_s.shape[0] - tm
            ext_s[hrows:hrows + tm, c0:c1] = cur
            conv = cur * w_ref[width - 1:width, c0:c1]
            for j in range(width - 1):
                d = (width - 1 - j) * shift
                conv = conv + ext_s[hrows - d:hrows - d + tm, c0:c1] * w_ref[j:j + 1, c0:c1]
            return conv

    def narrow_dots():
        return (proj(OFF_NARROW, 0, NARROW),)

    def narrow_epi(nar):
        lane = lax.broadcasted_iota(jnp.int32, nar.shape, 1)
        beta = jax.nn.sigmoid(nar)
        gdec = -jnp.exp(hp_ref[0:1, :]) * _softplus(nar + hp_ref[1:2, :])
        nar_ref[...] = jnp.where(lane < N_HEADS, beta, jnp.where(lane < 2 * N_HEADS, gdec, 0.0))

    def branch_a_dots(c0):
        return tuple(proj(off, c0) for off in (OFF_B, OFF_C, OFF_H, OFF_Z))

    def branch_a_epi(c0, pb, pc, ph, pz):
        c1 = c0 + COL_CHUNK
        ch = pc * ph
        if has_state:
            sa_ref[:, c0:c1] = ch
        elif grouped:
            for i, r in enumerate(last_rows(ch, na)):
                sa_ref[i:i + 1, c0:c1] = r
        conv = causal_conv(exta_s, CONV_A_W, caw_ref, ch, c0, c1)
        pre_s[:, c0:c1] = (_silu(pz) * pb * conv).astype(BF16)

    def gate_a_dots(c0):
        return (_dot(pre_s[...], woa_ref[:, c0:c0 + COL_CHUNK]), proj(OFF_GA, c0))

    def gate_a_epi(c0, ya, ga):
        ma_ref[:, c0:c0 + COL_CHUNK] = (jax.nn.sigmoid(ga) * ya).astype(ma_ref.dtype)

    def qkv_dots(e0):
        return (proj(OFF_QKV, e0),)

    def qkv_epi(e0, p):
        g, c0 = divmod(e0, D_MODEL)
        out_ref = (q_ref, k_ref, v_ref)[g]
        if has_state:
            sq_ref[:, e0:e0 + COL_CHUNK] = p
        elif grouped:
            for i, r in enumerate(last_rows(p, nb)):
                sq_ref[i:i + 1, e0:e0 + COL_CHUNK] = r
        s = _silu(causal_conv(extb_s, CONV_B_W, cbw_ref, p, e0, e0 + COL_CHUNK))
        if g == 2:
            out_ref[:, c0:c0 + COL_CHUNK] = s.astype(out_ref.dtype)
            return
        scale = HEAD_DIM ** -0.5 if g == 0 else 1.0
        for h0 in range(0, COL_CHUNK, HEAD_DIM):
            sh = s[:, h0:h0 + HEAD_DIM]
            ss = jnp.sum(sh * sh, axis=-1, keepdims=True)
            out_ref[:, c0 + h0:c0 + h0 + HEAD_DIM] = (
                sh * (lax.rsqrt(ss + EPS) * scale)).astype(out_ref.dtype)

    def act_dots(off, c0):
        return (proj(off, c0),)

    def zb_epi(c0, p):
        zb_ref[:, c0:c0 + COL_CHUNK] = _silu(p).astype(zb_ref.dtype)

    def sgb_epi(c0, p):
        sgb_ref[:, c0:c0 + COL_CHUNK] = jax.nn.sigmoid(p).astype(sgb_ref.dtype)

    part = functools.partial
    cols = list(range(0, D_MODEL, COL_CHUNK))
    n_col = len(cols)
    stages = [[(part(qkv_dots, e0), part(qkv_epi, e0))] for e0 in range(0, QKV_W, COL_CHUNK)]
    for n, c0 in enumerate(cols):
        stages[2 * n].append((part(branch_a_dots, c0), part(branch_a_epi, c0)))
        stages[2 * n + 1] += [(part(act_dots, OFF_ZB, c0), part(zb_epi, c0)),
                              (part(act_dots, OFF_GB, c0), part(sgb_epi, c0))]
        stages[2 * n_col + n].append((part(gate_a_dots, c0), part(gate_a_epi, c0)))
    stages[2 * n_col].append((narrow_dots, narrow_epi))
    assert len(stages) == 3 * n_col

    def run_dots(i):
        widths, k = [], 0
        for dots, _ in stages[i]:
            outs = dots()
            for r in outs:
                stage_s[i % 2][k, :, 0:r.shape[1]] = r
                k += 1
            widths.append([r.shape[1] for r in outs])
        return widths

    widths = run_dots(0)
    for i, stage in enumerate(stages):
        ready = widths
        if i + 1 < len(stages):
            widths = run_dots(i + 1)
        k = 0
        for (_, epilogue), ws in zip(stage, ready):
            epilogue(*[stage_s[i % 2][k + n, :, 0:w] for n, w in enumerate(ws)])
            k += len(ws)
    if not has_state and not grouped:
        sa_ref[...] = exta_s[ha + tm - na:ha + tm, :]
        sq_ref[...] = extb_s[hb + tm - nb:hb + tm, :]


def _resident(shape):
    nd = len(shape)
    return pl.BlockSpec(shape, lambda *_: (0,) * nd, pipeline_mode=pl.Buffered(1))


def _front(x2d, consts, *, tm, shift, n_seq, state=None):
    rows = x2d.shape[0]
    n_tiles = rows // tm
    tiles_per_seq = n_tiles // n_seq
    na = (CONV_A_W - 1) * shift
    nb = (CONV_B_W - 1) * shift
    hdr_a = -(-na // 8) * 8
    hdr_b = -(-nb // 8) * 8
    has_state = state is not None
    nw, wm, wt, caw, cbw, hp, woa = consts

    row_spec = lambda w: pl.BlockSpec((tm, w), lambda i: (i, 0))
    in_specs = [row_spec(D_MODEL), _resident(nw.shape), _resident(wm.shape), _resident(wt.shape),
                _resident(caw.shape), _resident(cbw.shape), _resident(hp.shape),
                _resident(woa.shape)]
    args = [x2d, nw, wm, wt, caw, cbw, hp, woa]
    out_specs = [row_spec(D_MODEL)] * 6 + [row_spec(NARROW)]
    out_shape = [jax.ShapeDtypeStruct((rows, D_MODEL), BF16)] * 6 + [
        jax.ShapeDtypeStruct((rows, NARROW), F32)]
    if has_state:
        sta, stq = state
        assert sta.shape[0] == hdr_a == na and stq.shape[0] == hdr_b == nb
        in_specs += [_resident(sta.shape), _resident(stq.shape)]
        args += [sta, stq]
        out_specs += [row_spec(D_MODEL), row_spec(QKV_W)]
        out_shape += [jax.ShapeDtypeStruct((rows, D_MODEL), F32),
                      jax.ShapeDtypeStruct((rows, QKV_W), F32)]
    else:
        seq_spec = lambda r, w: pl.BlockSpec((None, r, w), lambda i: (i // tiles_per_seq, 0, 0))
        out_specs += [seq_spec(na, D_MODEL), seq_spec(nb, QKV_W)]
        out_shape += [jax.ShapeDtypeStruct((n_seq, na, D_MODEL), F32),
                      jax.ShapeDtypeStruct((n_seq, nb, QKV_W), F32)]
    if shift == 1:
        assert tm % GROUP == 0
        ext_rows = (8 * (CONV_A_W - 1), 8 * (CONV_B_W - 1))
    else:
        ext_rows = (hdr_a + tm, hdr_b + tm)
    scratch = [pltpu.VMEM((tm, D_MODEL), BF16),
               pltpu.VMEM((ext_rows[0], D_MODEL), F32),
               pltpu.VMEM((ext_rows[1], QKV_W), F32),
               pltpu.VMEM((tm, D_MODEL), BF16),
               pltpu.VMEM((5, tm, COL_CHUNK), F32), pltpu.VMEM((5, tm, COL_CHUNK), F32)]
    return pl.pallas_call(
        functools.partial(_front_kernel, tm, shift, tiles_per_seq, has_state),
        grid=(n_tiles,),
        in_specs=in_specs,
        out_specs=out_specs,
        out_shape=out_shape,
        scratch_shapes=scratch,
        compiler_params=pltpu.CompilerParams(
            dimension_semantics=("arbitrary",), vmem_limit_bytes=V7X_VMEM_LIMIT),
        name="front",
    )(*args)


def _delta_kernel(chunk, cpb, spb, n_blocks, blocks_per_seq, has_init, fused_back, grouped,
                  *refs):
    refs = list(refs)
    q_ref, k_ref, v_ref, nar_ref = refs[:4]
    del refs[:4]
    s0_ref = refs.pop(0) if has_init else None
    if fused_back:
        back_in = refs[:8]
        del refs[:8]
    o_ref, so_ref, st_s, up_s, wdq_s, kdt_s, aqk_s, dl_s, tp_s = refs[:9]
    C = chunk
    nch = spb * cpb
    levels = int(math.log2(C))
    grp = min(N_HEADS, 256 // C)
    W = grp * C
    sdt = wdq_s.dtype
    mx = lambda a: a.astype(BF16)
    step = pl.program_id(0)
    block = jnp.minimum(step, n_blocks - 1)
    fresh = (block % blocks_per_seq) == 0

    if fused_back:
        o_s, on_s, m_s = refs[9:]
        y_ref, o_dst = o_ref, o_s

        @pl.when(step == 0)
        def _():
            st_s[...] = jnp.zeros(st_s.shape, F32)
            o_s[...] = jnp.zeros(o_s.shape, F32)

        for s in range(spb):
            _back_kernel(back_in[0].at[s], o_s.at[s], *[r.at[s] for r in back_in[1:4]],
                         *back_in[4:], y_ref.at[s], on_s, m_s, grouped=grouped)
    else:
        o_dst = o_ref

        @pl.when(step == 0)
        def _():
            st_s[...] = jnp.zeros(st_s.shape, F32)

    row = lax.broadcasted_iota(jnp.int32, (C, C), 0)
    col = lax.broadcasted_iota(jnp.int32, (C, C), 1)
    row_t, col_t = (_group_time(row), _group_time(col)) if grouped else (row, col)
    causal = row_t >= col_t
    strict = row_t > col_t
    ltri = causal.astype(BF16)
    utri = (row_t <= col_t).astype(BF16)
    rw = lax.broadcasted_iota(jnp.int32, (W, W), 0)
    cw = lax.broadcasted_iota(jnp.int32, (W, W), 1)
    blockdiag = (rw >> levels) == (cw >> levels)
    ri = lax.broadcasted_iota(jnp.int32, (C, W), 0)
    ci = lax.broadcasted_iota(jnp.int32, (C, W), 1)
    eye_cat = (ri == (ci & (C - 1))).astype(F32)
    groups = [(c, g) for c in range(nch) for g in range(N_HEADS // grp)]

    def cumsum_t(a, tri, dims):
        hi = a.astype(BF16)
        lo = (a - hi.astype(F32)).astype(BF16)
        dn = (dims, ((), ()))
        if dims[0] == (1,):
            return (lax.dot_general(tri, hi, dn, preferred_element_type=F32)
                    + lax.dot_general(tri, lo, dn, preferred_element_type=F32))
        return (lax.dot_general(hi, tri, dn, preferred_element_type=F32)
                + lax.dot_general(lo, tri, dn, preferred_element_type=F32))

    heads = [(c, h) for c in range(nch) for h in range(N_HEADS)]
    hsl = lambda h: slice(h * HEAD_DIM, (h + 1) * HEAD_DIM)

    def blk(ref, c, cols=slice(None)):
        s, cc = divmod(c, cpb)
        return ref[s, cc * C:(cc + 1) * C, cols]

    ident = (row == col).astype(BF16)
    gcs = [cumsum_t(blk(nar_ref, c), ltri, ((1,), (0,))) for c in range(nch)]
    gcts = [cumsum_t(blk(nar_ref, c), utri, ((0,), (0,))) for c in range(nch)]
    narts = [cumsum_t(blk(nar_ref, c), ident, ((0,), (0,))) for c in range(nch)]
    a_kq = {}
    for c, h in heads:
        kx = mx(blk(k_ref, c, hsl(h)))
        a_kq[c, h] = lax.dot_general(
            jnp.concatenate([kx, mx(blk(q_ref, c, hsl(h)))], axis=0), kx,
            (((1,), (1,)), ((), ())), preferred_element_type=F32)
    for c in range(nch):
        dl_s[c] = jnp.exp(gcs[c][C - 1:C, :])
    for c, h in heads:
        g, j = divmod(h, grp)
        gcw = jnp.broadcast_to(gcs[c][:, N_HEADS + h:N_HEADS + h + 1], (C, HEAD_DIM))
        betw = jnp.broadcast_to(blk(nar_ref, c, slice(h, h + 1)), (C, C))
        gcr = gcts[c][N_HEADS + h:N_HEADS + h + 1, :]
        decay = jnp.exp(jnp.where(causal, gcw[:, 0:C] - gcr, -jnp.inf))
        wdq_s[c, h, C:2 * C, :] = (blk(q_ref, c, hsl(h)).astype(F32) * jnp.exp(gcw)).astype(sdt)
        k_t = blk(k_ref, c, hsl(h)).astype(F32).T
        kdt_s[c, h] = (k_t * jnp.exp(gcr[:, C - 1:C] - gcr)).astype(sdt)
        a = a_kq[c, h]
        tp_s[c, g, C:2 * C, j * C:(j + 1) * C] = -jnp.where(strict, a[:C] * betw * decay, 0.0)
        aqk_s[c, h] = jnp.where(causal, a[C:] * decay, 0.0).astype(sdt)

    for c, g in groups:
        tp_s[c, g, 0:C, :] = eye_cat + tp_s[c, g, C:2 * C, :]
    for lv in range(levels):
        for c, g in groups:
            p = tp_s[c, g, C:2 * C, :]
            bd = mx(jnp.where(blockdiag, jnp.concatenate([p] * grp, axis=0), 0.0))
            if lv == 0:
                tp_s[c, g, C:2 * C, :] = _dot(mx(p), bd)
            elif lv < levels - 1:
                xx = _dot(mx(tp_s[c, g]), bd)
                tp_s[c, g, 0:C, :] = tp_s[c, g, 0:C, :] + xx[:C]
                tp_s[c, g, C:2 * C, :] = xx[C:]
            else:
                t = tp_s[c, g, 0:C, :]
                tp_s[c, g, 0:C, :] = t + _dot(mx(t), bd)

    for c, h in heads:
        g, j = divmod(h, grp)
        tb = tp_s[c, g, 0:C, j * C:(j + 1) * C] * narts[c][h:h + 1, :]
        tbe = tb * jnp.exp(gcts[c][N_HEADS + h:N_HEADS + h + 1, :])
        up_s[c, h] = _dot(mx(tb), mx(blk(v_ref, c, hsl(h))))
        wdq_s[c, h, 0:C, :] = _dot(mx(tbe), mx(blk(k_ref, c, hsl(h)))).astype(sdt)

    init = s0_ref[...] if has_init else jnp.zeros(st_s.shape, F32)
    st_s[...] = jnp.where(fresh, init, st_s[...])
    for cc in range(cpb):
        chains = [(s * cpb + cc, s, h) for s in range(spb) for h in range(N_HEADS)]
        rs = [_dot(mx(wdq_s[c, h]), mx(st_s[s, h])) for c, s, h in chains]
        us = [mx(up_s[c, h] - r[:C]) for (c, s, h), r in zip(chains, rs)]
        for (c, s, h), r, u in zip(chains, rs, us):
            o = r[C:] + _dot(mx(aqk_s[c, h]), u)
            o_dst[s, cc * C:(cc + 1) * C, hsl(h)] = o.astype(o_dst.dtype)
        for (c, s, h), u in zip(chains, us):
            dl = dl_s[c][:, N_HEADS + h:N_HEADS + h + 1]
            new = st_s[s, h] * dl + _dot(mx(kdt_s[c, h]), u)
            st_s[s, h] = jnp.where(step < n_blocks, new, st_s[s, h]) if fused_back else new

    so_ref[...] = st_s[...]


def _delta(q, k, v, nar, *, n_seq, chunk, cpb, spb=1, s0=None, back=None, grouped=False):
    rows = q.shape[0]
    seq_len = rows // n_seq
    per_seq = lambda a: a.reshape(n_seq, seq_len, a.shape[-1])
    tb = chunk * cpb
    blocks_per_seq = seq_len // tb
    n_blocks = (n_seq // spb) * blocks_per_seq
    has_init = s0 is not None
    fused_back = back is not None
    nch = spb * cpb
    cur = lambda s: jnp.minimum(s, n_blocks - 1)
    prev = lambda s: jnp.maximum(s - 1, 0)
    row_spec = lambda w, at: pl.BlockSpec(
        (spb, tb, w), lambda s: (at(s) // blocks_per_seq, at(s) % blocks_per_seq, 0))
    st_shape = (spb, N_HEADS, HEAD_DIM, HEAD_DIM)
    in_specs = [row_spec(D_MODEL, cur)] * 3 + [row_spec(NARROW, cur)]
    args = [per_seq(a) for a in (q, k, v, nar)]
    if has_init:
        in_specs.append(pl.BlockSpec(st_shape, lambda s: (cur(s) // blocks_per_seq, 0, 0, 0)))
        args.append(s0)
    sdt = BF16 if chunk % 16 == 0 else F32
    grp = min(N_HEADS, 256 // chunk)
    scratch = [pltpu.VMEM(st_shape, F32),
               pltpu.VMEM((nch, N_HEADS, chunk, HEAD_DIM), F32),
               pltpu.VMEM((nch, N_HEADS, 2 * chunk, HEAD_DIM), sdt),
               pltpu.VMEM((nch, N_HEADS, HEAD_DIM, chunk), sdt),
               pltpu.VMEM((nch, N_HEADS, chunk, chunk), sdt),
               pltpu.VMEM((nch, 1, NARROW), F32),
               pltpu.VMEM((nch, N_HEADS // grp, 2 * chunk, grp * chunk), F32)]
    if fused_back:
        x2d, zb, ma, sgb, consts = back
        in_specs += [row_spec(D_MODEL, prev)] * 4 + [_resident(c.shape) for c in consts]
        args += [per_seq(a) for a in (x2d, zb, ma, sgb)] + list(consts)
        out0 = (row_spec(D_MODEL, prev), jax.ShapeDtypeStruct((n_seq, seq_len, D_MODEL), F32))
        scratch += [pltpu.VMEM((spb, tb, D_MODEL), F32),
                    pltpu.VMEM((tb, D_MODEL), BF16), pltpu.VMEM((tb, D_MODEL), BF16)]
    else:
        out0 = (row_spec(D_MODEL, cur),
                jax.ShapeDtypeStruct((n_seq, seq_len, D_MODEL), q.dtype))
    out, state = pl.pallas_call(
        functools.partial(_delta_kernel, chunk, cpb, spb, n_blocks, blocks_per_seq, has_init,
                          fused_back, grouped),
        grid=(n_blocks + 1 if fused_back else n_blocks,),
        in_specs=in_specs,
        out_specs=[out0[0],
                   pl.BlockSpec(st_shape, lambda s: (cur(s) // blocks_per_seq, 0, 0, 0))],
        out_shape=[out0[1], jax.ShapeDtypeStruct((n_seq, N_HEADS, HEAD_DIM, HEAD_DIM), F32)],
        scratch_shapes=scratch,
        compiler_params=pltpu.CompilerParams(
            dimension_semantics=("arbitrary",), vmem_limit_bytes=V7X_VMEM_LIMIT),
        name="delta",
    )(*args)
    return out.reshape(rows, D_MODEL), state


def _back_kernel(x_ref, o_ref, zb_ref, ma_ref, sgb_ref, onw_ref, wob_ref, wo_ref, fw_ref,
                 y_ref, on_s, m_s, grouped=False):
    for h in range(N_HEADS):
        hs = slice(h * HEAD_DIM, (h + 1) * HEAD_DIM)
        oh = o_ref[:, hs].astype(F32)
        ms = jnp.mean(oh * oh, axis=-1, keepdims=True)
        on = oh * lax.rsqrt(ms + EPS) * onw_ref[...]
        on_s[:, hs] = (on * zb_ref[:, hs].astype(F32)).astype(BF16)
    for c0 in range(0, D_MODEL, COL_CHUNK):
        c1 = c0 + COL_CHUNK
        yb = _dot(on_s[...], wob_ref[:, c0:c1])
        m = ma_ref[:, c0:c1].astype(F32) + sgb_ref[:, c0:c1].astype(F32) * yb
        m_s[:, c0:c1] = m.astype(BF16)
    m = m_s[...]
    if grouped:
        m = _dot(_group_perm(m.shape[0]), m).astype(BF16)
    hres = x_ref[...] + _dot(m, wo_ref[...])
    var = jnp.mean(hres * hres, axis=-1, keepdims=True)
    y_ref[...] = hres * lax.rsqrt(var + EPS) * fw_ref[...]


def _back(x2d, o, zb, ma, sgb, consts, *, tm):
    rows = x2d.shape[0]
    onw, wob, wo, fw = consts
    row_spec = pl.BlockSpec((tm, D_MODEL), lambda i: (i, 0))
    return pl.pallas_call(
        _back_kernel,
        grid=(rows // tm,),
        in_specs=[row_spec] * 5 + [_resident(onw.shape), _resident(wob.shape),
                                   _resident(wo.shape), _resident(fw.shape)],
        out_specs=row_spec,
        out_shape=jax.ShapeDtypeStruct((rows, D_MODEL), F32),
        scratch_shapes=[pltpu.VMEM((tm, D_MODEL), BF16), pltpu.VMEM((tm, D_MODEL), BF16)],
        compiler_params=pltpu.CompilerParams(
            dimension_semantics=("arbitrary",), vmem_limit_bytes=V7X_VMEM_LIMIT),
        name="back",
    )(x2d, o, zb, ma, sgb, onw, wob, wo, fw)


def kernel(x_prompt, x_sample, state_conv_a, state_conv_qkv, state_delta, w_in, conv_a_w,
           conv_b_w, a_log, dt_bias, onorm_w, w_out_a, w_out_b, w_o, norm_w, final_norm_w):
    assert w_in.shape[0] == 1, "single layer"
    bp, tp, _ = x_prompt.shape
    bs, ts, _ = x_sample.shape
    wm = w_in[0].astype(BF16)
    n_wide = OFF_ZB + D_MODEL
    wn = jnp.pad(wm[:, n_wide:n_wide + 2 * N_HEADS], ((0, 0), (0, NARROW - 2 * N_HEADS)))
    wt = jnp.concatenate([wm[:, n_wide + 2 * N_HEADS:], wn], axis=1)
    hp = jnp.zeros((8, NARROW), F32)
    hp = hp.at[0, N_HEADS:2 * N_HEADS].set(a_log[0]).at[1, N_HEADS:2 * N_HEADS].set(dt_bias[0])
    front_consts = (norm_w[0][None, :], wm, wt, conv_a_w[0], conv_b_w[0], hp,
                    w_out_a[0].astype(BF16))
    back_consts = (onorm_w[0][None, :], w_out_b[0].astype(BF16), w_o[0].astype(BF16),
                   final_norm_w[None, :])

    xp = x_prompt.reshape(bp * tp, D_MODEL)
    ma, q, k, v, zb, sgb, nar, sa, sq = _front(xp, front_consts, tm=256, shift=1, n_seq=bp)
    y_prompt, s_new = _delta(q, k, v, nar, n_seq=bp, chunk=PROMPT_CHUNK, cpb=4, spb=2, grouped=True,
                             back=(xp, zb, ma, sgb, back_consts))
    y_prompt = y_prompt.reshape(bp, tp, D_MODEL)
    new_conv_a_prompt = sa[None]
    new_conv_qkv_prompt = sq[None]
    new_delta_prompt = s_new[None]

    xs = jnp.transpose(x_sample, (1, 0, 2)).reshape(ts * bs, D_MODEL)
    sta = jnp.transpose(state_conv_a[0], (1, 0, 2)).reshape((CONV_A_W - 1) * bs, D_MODEL)
    stq = jnp.transpose(state_conv_qkv[0], (1, 0, 2)).reshape((CONV_B_W - 1) * bs, QKV_W)
    ma, q, k, v, zb, sgb, nar, cha, pqkv = _front(
        xs, front_consts, tm=bs, shift=bs, n_seq=1, state=(sta, stq))
    sa = jnp.concatenate([sta, cha], axis=0)[ts * bs:]
    sq = jnp.concatenate([stq, pqkv], axis=0)[ts * bs:]
    tpad = 8

    def to_batch_major(a):
        a = jnp.transpose(a.reshape(ts, bs, a.shape[-1]), (1, 0, 2)).astype(F32)
        return jnp.pad(a, ((0, 0), (0, tpad - ts), (0, 0))).reshape(bs * tpad, a.shape[-1])

    o, s_new = _delta(to_batch_major(q), to_batch_major(k), to_batch_major(v),
                      to_batch_major(nar), n_seq=bs, chunk=tpad, cpb=1, spb=8, s0=state_delta[0])
    o = jnp.transpose(o.reshape(bs, tpad, D_MODEL)[:, :ts], (1, 0, 2)).reshape(ts * bs, D_MODEL)
    ys = _back(xs, o, zb, ma, sgb, back_consts, tm=ts * bs)
    y_sample = jnp.transpose(ys.reshape(ts, bs, D_MODEL), (1, 0, 2))
    new_conv_a_sample = jnp.transpose(sa.reshape(CONV_A_W - 1, bs, D_MODEL), (1, 0, 2))[None]
    new_conv_qkv_sample = jnp.transpose(sq.reshape(CONV_B_W - 1, bs, QKV_W), (1, 0, 2))[None]
    new_delta_sample = s_new[None]

    return (y_prompt, y_sample, new_conv_a_prompt, new_conv_qkv_prompt, new_delta_prompt,
            new_conv_a_sample, new_conv_qkv_sample, new_delta_sample)
```

```python
import functools
import math

import jax
import jax.numpy as jnp
from jax import lax
from jax.experimental import pallas as pl
from jax.experimental.pallas import tpu as pltpu

D_MODEL = 1024
N_HEADS = 8
HEAD_DIM = 128
QKV_W = 3 * D_MODEL
CONV_A_W = 3
CONV_B_W = 4
PROMPT_CHUNK = 64
EPS = 1e-6
NARROW = 128
OFF_B, OFF_C, OFF_H, OFF_Z, OFF_QKV, OFF_ZB, OFF_GA, OFF_GB, OFF_NARROW = (
    0, 1024, 2048, 3072, 4096, 7168, 8192, 9216, 10240)

V7X_VMEM_LIMIT = 56 * 1024 * 1024
COL_CHUNK = 256

F32 = jnp.float32
BF16 = jnp.bfloat16


def _dot(a, b):
    return jnp.dot(a, b, preferred_element_type=F32)


def _silu(x):
    return x * jax.nn.sigmoid(x)


GROUP = PROMPT_CHUNK


def _group_time(r):
    return ((r & 7) << 3) | ((r >> 3) & 7)


def _group_perm(n):
    r = lax.broadcasted_iota(jnp.int32, (n, n), 0)
    c = lax.broadcasted_iota(jnp.int32, (n, n), 1)
    return (c == ((r & ~(GROUP - 1)) | _group_time(r))).astype(BF16)


def _softplus(x):
    return jnp.maximum(x, 0.0) + jnp.log(1.0 + jnp.exp(-jnp.abs(x)))


def _front_kernel(tm, shift, tiles_per_seq, has_state, *refs):
    if has_state:
        (x_ref, nw_ref, wm_ref, wt_ref, caw_ref, cbw_ref, hp_ref, woa_ref, sta_ref, stq_ref,
         ma_ref, q_ref, k_ref, v_ref, zb_ref, sgb_ref, nar_ref, sa_ref, sq_ref,
         xn_s, exta_s, extb_s, pre_s) = refs
    else:
        (x_ref, xnext_ref, nw_ref, wm_ref, wt_ref, caw_ref, cbw_ref, hp_ref, woa_ref,
         ma_ref, q_ref, k_ref, v_ref, zb_ref, sgb_ref, nar_ref, sa_ref, sq_ref,
         xn_s, exta_s, extb_s, pre_s, xn_next_s) = refs
        sta_ref = stq_ref = None
    grouped = shift == 1
    na = (CONV_A_W - 1) * shift
    nb = (CONV_B_W - 1) * shift
    first = (pl.program_id(0) % tiles_per_seq) == 0

    def normed(ref):
        x = ref[...]
        var = jnp.mean(x * x, axis=-1, keepdims=True)
        xn = (x * lax.rsqrt(var + EPS) * nw_ref[...]).astype(BF16)
        return _dot(_group_perm(tm), xn).astype(BF16) if grouped else xn

    if has_state:
        xn_s[...] = normed(x_ref)
    else:
        @pl.when(pl.program_id(0) == 0)
        def _():
            xn_next_s[...] = normed(x_ref)

        xn_s[...] = xn_next_s[...]

    def proj(off, c0, width=COL_CHUNK):
        w_ref, lo = (wm_ref, off + c0) if off < OFF_GA else (wt_ref, off - OFF_GA + c0)
        return _dot(xn_s[...], w_ref[:, lo:lo + width])

    if grouped:
        assert not has_state

        @pl.when(pl.program_id(0) == 0)
        def _():
            exta_s[...] = jnp.zeros(exta_s.shape, F32)
            extb_s[...] = jnp.zeros(extb_s.shape, F32)

        for ext_s in (exta_s, extb_s):
            ext_s[...] = jnp.where(first, 0.0, ext_s[...])

        def causal_conv(ext_s, width, w_ref, cur, c0, c1):
            taps = width - 1
            sub0 = lax.broadcasted_iota(jnp.int32, (8, c1 - c0), 0) == 0
            prev_tail = [ext_s[8 * i:8 * i + 8, c0:c1] for i in range(taps)]
            shifted = [[] for _ in range(taps)]
            for g0 in range(0, tm, GROUP):
                vrow = lambda v: cur[g0 + 8 * v:g0 + 8 * v + 8]
                tail = [vrow(8 - taps + i) for i in range(taps)]
                wrap = [jnp.where(sub0, pltpu.roll(p, 1, 0), pltpu.roll(t, 1, 0))
                        for p, t in zip(prev_tail, tail)]
                for d in range(1, taps + 1):
                    shifted[d - 1] += [vrow(v - d) if v >= d else wrap[v - d + taps]
                                       for v in range(8)]
                prev_tail = tail
            for i in range(taps):
                ext_s[8 * i:8 * i + 8, c0:c1] = prev_tail[i]
            conv = cur * w_ref[width - 1:width, c0:c1]
            for d in range(1, taps + 1):
                conv = conv + (jnp.concatenate(shifted[d - 1], axis=0)
                               * w_ref[width - 1 - d:width - d, c0:c1])
            return conv

        def last_rows(cur, n):
            rows = [tm - GROUP + _group_time(t) for t in range(GROUP - n, GROUP)]
            return [cur[r:r + 1] for r in rows]
    else:
        ha = exta_s.shape[0] - tm
        hb = extb_s.shape[0] - tm

        @pl.when(pl.program_id(0) == 0)
        def _():
            exta_s[tm:tm + ha, :] = jnp.zeros((ha, D_MODEL), F32)
            extb_s[tm:tm + hb, :] = jnp.zeros((hb, QKV_W), F32)

        for ext_s, st_ref, hrows in ((exta_s, sta_ref, ha), (extb_s, stq_ref, hb)):
            for c0 in range(0, ext_s.shape[1], COL_CHUNK):
                cs = slice(c0, c0 + COL_CHUNK)
                init = st_ref[:, cs] if has_state else jnp.zeros((hrows, COL_CHUNK), F32)
                ext_s[0:hrows, cs] = jnp.where(first, init, ext_s[tm:tm + hrows, cs])

        def causal_conv(ext_s, width, w_ref, cur, c0, c1):
            hrows = ext_s.shape[0] - tm
            ext_s[hrows:hrows + tm, c0:c1] = cur
            conv = cur * w_ref[width - 1:width, c0:c1]
            for j in range(width - 1):
                d = (width - 1 - j) * shift
                conv = conv + ext_s[hrows - d:hrows - d + tm, c0:c1] * w_ref[j:j + 1, c0:c1]
            return conv

    def narrow_dots():
        return (proj(OFF_NARROW, 0, NARROW),)

    def narrow_epi(nar):
        lane = lax.broadcasted_iota(jnp.int32, nar.shape, 1)
        beta = jax.nn.sigmoid(nar)
        gdec = -jnp.exp(hp_ref[0:1, :]) * _softplus(nar + hp_ref[1:2, :])
        nar_ref[...] = jnp.where(lane < N_HEADS, beta, jnp.where(lane < 2 * N_HEADS, gdec, 0.0))

    def branch_a_dots(c0):
        return tuple(proj(off, c0) for off in (OFF_B, OFF_C, OFF_H, OFF_Z))

    def branch_a_epi(c0, pb, pc, ph, pz):
        c1 = c0 + COL_CHUNK
        ch = pc * ph
        if has_state:
            sa_ref[:, c0:c1] = ch
        elif grouped:
            for i, r in enumerate(last_rows(ch, na)):
                sa_ref[i:i + 1, c0:c1] = r
        conv = causal_conv(exta_s, CONV_A_W, caw_ref, ch, c0, c1)
        pre_s[:, c0:c1] = (_silu(pz) * pb * conv).astype(BF16)

    def gate_a_dots(c0):
        return (_dot(pre_s[...], woa_ref[:, c0:c0 + COL_CHUNK]), proj(OFF_GA, c0))

    def gate_a_epi(c0, ya, ga):
        ma_ref[:, c0:c0 + COL_CHUNK] = (jax.nn.sigmoid(ga) * ya).astype(ma_ref.dtype)

    def qkv_dots(e0):
        return (proj(OFF_QKV, e0),)

    def qkv_epi(e0, p):
        g, c0 = divmod(e0, D_MODEL)
        out_ref = (q_ref, k_ref, v_ref)[g]
        if has_state:
            sq_ref[:, e0:e0 + COL_CHUNK] = p
        elif grouped:
            for i, r in enumerate(last_rows(p, nb)):
                sq_ref[i:i + 1, e0:e0 + COL_CHUNK] = r
        s = _silu(causal_conv(extb_s, CONV_B_W, cbw_ref, p, e0, e0 + COL_CHUNK))
        if g == 2:
            out_ref[:, c0:c0 + COL_CHUNK] = s.astype(out_ref.dtype)
            return
        scale = HEAD_DIM ** -0.5 if g == 0 else 1.0
        for h0 in range(0, COL_CHUNK, HEAD_DIM):
            sh = s[:, h0:h0 + HEAD_DIM]
            ss = jnp.sum(sh * sh, axis=-1, keepdims=True)
            out_ref[:, c0 + h0:c0 + h0 + HEAD_DIM] = (
                sh * (lax.rsqrt(ss + EPS) * scale)).astype(out_ref.dtype)

    def act_dots(off, c0):
        return (proj(off, c0),)

    def zb_epi(c0, p):
        zb_ref[:, c0:c0 + COL_CHUNK] = _silu(p).astype(zb_ref.dtype)

    def sgb_epi(c0, p):
        sgb_ref[:, c0:c0 + COL_CHUNK] = jax.nn.sigmoid(p).astype(sgb_ref.dtype)

    part = functools.partial
    cols = list(range(0, D_MODEL, COL_CHUNK))
    n_col = len(cols)
    stages = [[(part(qkv_dots, e0), part(qkv_epi, e0))] for e0 in range(0, QKV_W, COL_CHUNK)]
    for n, c0 in enumerate(cols):
        stages[2 * n].append((part(branch_a_dots, c0), part(branch_a_epi, c0)))
        stages[2 * n + 1] += [(part(act_dots, OFF_ZB, c0), part(zb_epi, c0)),
                              (part(act_dots, OFF_GB, c0), part(sgb_epi, c0))]
        stages[2 * n_col + n].append((part(gate_a_dots, c0), part(gate_a_epi, c0)))
    stages[2 * n_col].append((narrow_dots, narrow_epi))
    assert len(stages) == 3 * n_col

    def run_dots(i):
        return [dots() for dots, _ in stages[i]]

    pending = run_dots(0)
    for i, stage in enumerate(stages):
        ready = pending
        if i + 1 < len(stages):
            pending = run_dots(i + 1)
        elif not has_state:
            xn_next_s[...] = normed(xnext_ref)
        for (_, epilogue), outs in zip(stage, ready):
            epilogue(*outs)
    if not has_state and not grouped:
        sa_ref[...] = exta_s[ha + tm - na:ha + tm, :]
        sq_ref[...] = extb_s[hb + tm - nb:hb + tm, :]


def _resident(shape):
    nd = len(shape)
    return pl.BlockSpec(shape, lambda *_: (0,) * nd, pipeline_mode=pl.Buffered(1))


def _front(x, consts, *, tm, shift, n_seq, state=None):
    if x.ndim == 3:
        assert x.shape[0] == tm == shift
        rows = x.shape[0] * x.shape[1]
        x = x.reshape(tm, -1)
        x_spec = pl.BlockSpec((tm, D_MODEL), lambda i: (0, i))
    else:
        rows = x.shape[0]
        x_spec = pl.BlockSpec((tm, D_MODEL), lambda i: (i, 0))
    n_tiles = rows // tm
    tiles_per_seq = n_tiles // n_seq
    na = (CONV_A_W - 1) * shift
    nb = (CONV_B_W - 1) * shift
    hdr_a = -(-na // 8) * 8
    hdr_b = -(-nb // 8) * 8
    has_state = state is not None
    nw, wm, wt, caw, cbw, hp, woa = consts

    row_spec = lambda w: pl.BlockSpec((tm, w), lambda i: (i, 0))
    in_specs = [x_spec, _resident(nw.shape), _resident(wm.shape), _resident(wt.shape),
                _resident(caw.shape), _resident(cbw.shape), _resident(hp.shape),
                _resident(woa.shape)]
    args = [x, nw, wm, wt, caw, cbw, hp, woa]
    if not has_state:
        in_specs.insert(1, pl.BlockSpec((tm, D_MODEL), lambda i: (jnp.minimum(i + 1, n_tiles - 1), 0)))
        args.insert(1, x)
    out_specs = [row_spec(D_MODEL)] * 6 + [row_spec(NARROW)]
    out_shape = [jax.ShapeDtypeStruct((rows, D_MODEL), BF16)] * 6 + [
        jax.ShapeDtypeStruct((rows, NARROW), F32)]
    if has_state:
        sta, stq = state
        assert sta.shape[0] == hdr_a == na and stq.shape[0] == hdr_b == nb
        in_specs += [_resident(sta.shape), _resident(stq.shape)]
        args += [sta, stq]
        out_specs += [row_spec(D_MODEL), row_spec(QKV_W)]
        out_shape += [jax.ShapeDtypeStruct((rows, D_MODEL), F32),
                      jax.ShapeDtypeStruct((rows, QKV_W), F32)]
    else:
        seq_spec = lambda r, w: pl.BlockSpec((None, r, w), lambda i: (i // tiles_per_seq, 0, 0))
        out_specs += [seq_spec(na, D_MODEL), seq_spec(nb, QKV_W)]
        out_shape += [jax.ShapeDtypeStruct((n_seq, na, D_MODEL), F32),
                      jax.ShapeDtypeStruct((n_seq, nb, QKV_W), F32)]
    if shift == 1:
        assert tm % GROUP == 0
        ext_rows = (8 * (CONV_A_W - 1), 8 * (CONV_B_W - 1))
    else:
        ext_rows = (hdr_a + tm, hdr_b + tm)
    scratch = [pltpu.VMEM((tm, D_MODEL), BF16),
               pltpu.VMEM((ext_rows[0], D_MODEL), F32),
               pltpu.VMEM((ext_rows[1], QKV_W), F32),
               pltpu.VMEM((tm, D_MODEL), BF16)]
    if not has_state:
        scratch.append(pltpu.VMEM((tm, D_MODEL), BF16))
    return pl.pallas_call(
        functools.partial(_front_kernel, tm, shift, tiles_per_seq, has_state),
        grid=(n_tiles,),
        in_specs=in_specs,
        out_specs=out_specs,
        out_shape=out_shape,
        scratch_shapes=scratch,
        compiler_params=pltpu.CompilerParams(
            dimension_semantics=("arbitrary",), vmem_limit_bytes=V7X_VMEM_LIMIT),
        name="front",
    )(*args)


def _delta_kernel(chunk, cpb, spb, n_blocks, blocks_per_seq, has_init, fused_back, grouped,
                  live_rows, *refs):
    refs = list(refs)
    q_ref, k_ref, v_ref, nar_ref = refs[:4]
    del refs[:4]
    s0_ref = refs.pop(0) if has_init else None
    if fused_back:
        back_in = refs[:8]
        del refs[:8]
    o_ref, so_ref, st_s, up_s, wdq_s, kdt_s, aqk_s, dl_s, tp_s = refs[:9]
    C = chunk
    nch = spb * cpb
    log_c = int(math.log2(C))
    levels = max(2, math.ceil(math.log2(live_rows)))
    carried = blocks_per_seq > 1 or cpb > 1
    grp = min(N_HEADS, 256 // C)
    W = grp * C
    sdt = wdq_s.dtype
    mx = lambda a: a.astype(BF16)
    step = pl.program_id(0)
    block = jnp.minimum(step, n_blocks - 1)
    fresh = (block % blocks_per_seq) == 0

    if fused_back:
        o_s, on_s, m_s = refs[9:]
        y_ref, o_dst = o_ref, o_s

        assert carried

        @pl.when(step == 0)
        def _():
            st_s[...] = jnp.zeros(st_s.shape, F32)
            o_s[...] = jnp.zeros(o_s.shape, F32)

        for s in range(spb):
            _back_kernel(back_in[0].at[s], o_s.at[s], *[r.at[s] for r in back_in[1:4]],
                         *back_in[4:], y_ref.at[s], on_s, m_s, grouped=grouped)
    else:
        o_dst = o_ref
        if carried:
            @pl.when(step == 0)
            def _():
                st_s[...] = jnp.zeros(st_s.shape, F32)

    row = lax.broadcasted_iota(jnp.int32, (C, C), 0)
    col = lax.broadcasted_iota(jnp.int32, (C, C), 1)
    row_t, col_t = (_group_time(row), _group_time(col)) if grouped else (row, col)
    causal = row_t >= col_t
    strict = row_t > col_t
    ltri = causal.astype(BF16)
    utri = (row_t <= col_t).astype(BF16)
    rw = lax.broadcasted_iota(jnp.int32, (W, W), 0)
    cw = lax.broadcasted_iota(jnp.int32, (W, W), 1)
    blockdiag = (rw >> log_c) == (cw >> log_c)
    ri = lax.broadcasted_iota(jnp.int32, (C, W), 0)
    ci = lax.broadcasted_iota(jnp.int32, (C, W), 1)
    eye_cat = (ri == (ci & (C - 1))).astype(F32)
    groups = [(c, g) for c in range(nch) for g in range(N_HEADS // grp)]

    def cumsum_t(a, tri, dims):
        hi = a.astype(BF16)
        lo = (a - hi.astype(F32)).astype(BF16)
        dn = (dims, ((), ()))
        if dims[0] == (1,):
            return (lax.dot_general(tri, hi, dn, preferred_element_type=F32)
                    + lax.dot_general(tri, lo, dn, preferred_element_type=F32))
        return (lax.dot_general(hi, tri, dn, preferred_element_type=F32)
                + lax.dot_general(lo, tri, dn, preferred_element_type=F32))

    heads = [(c, h) for c in range(nch) for h in range(N_HEADS)]
    hsl = lambda h: slice(h * HEAD_DIM, (h + 1) * HEAD_DIM)

    def blk(ref, c, cols=slice(None)):
        s, cc = divmod(c, cpb)
        return ref[s, cc * C:(cc + 1) * C, cols]

    ident = (row == col).astype(BF16)
    gcs = [cumsum_t(blk(nar_ref, c), ltri, ((1,), (0,))) for c in range(nch)]
    gcts = [cumsum_t(blk(nar_ref, c), utri, ((0,), (0,))) for c in range(nch)]
    narts = [cumsum_t(blk(nar_ref, c), ident, ((0,), (0,))) for c in range(nch)]
    a_kq = {}
    for c, h in heads:
        kx = mx(blk(k_ref, c, hsl(h)))
        a_kq[c, h] = lax.dot_general(
            jnp.concatenate([kx, mx(blk(q_ref, c, hsl(h)))], axis=0), kx,
            (((1,), (1,)), ((), ())), preferred_element_type=F32)
    for c in range(nch):
        dl_s[c] = jnp.exp(gcs[c][C - 1:C, :])
    for c, h in heads:
        g, j = divmod(h, grp)
        gcw = jnp.broadcast_to(gcs[c][:, N_HEADS + h:N_HEADS + h + 1], (C, HEAD_DIM))
        betw = jnp.broadcast_to(blk(nar_ref, c, slice(h, h + 1)), (C, C))
        gcr = gcts[c][N_HEADS + h:N_HEADS + h + 1, :]
        decay = jnp.exp(jnp.where(causal, gcw[:, 0:C] - gcr, -jnp.inf))
        wdq_s[c, h, C:2 * C, :] = (blk(q_ref, c, hsl(h)).astype(F32) * jnp.exp(gcw)).astype(sdt)
        k_t = blk(k_ref, c, hsl(h)).astype(F32).T
        kdt_s[c, h] = (k_t * jnp.exp(gcr[:, C - 1:C] - gcr)).astype(sdt)
        a = a_kq[c, h]
        tp_s[c, g, C:2 * C, j * C:(j + 1) * C] = -jnp.where(strict, a[:C] * betw * decay, 0.0)
        aqk_s[c, h] = jnp.where(causal, a[C:] * decay, 0.0).astype(sdt)

    for c, g in groups:
        tp_s[c, g, 0:C, :] = eye_cat + tp_s[c, g, C:2 * C, :]
    for lv in range(levels):
        for c, g in groups:
            p = tp_s[c, g, C:2 * C, :]
            bd = mx(jnp.where(blockdiag, jnp.concatenate([p] * grp, axis=0), 0.0))
            if lv == 0:
                tp_s[c, g, C:2 * C, :] = _dot(mx(p), bd)
            elif lv < levels - 1:
                xx = _dot(mx(tp_s[c, g]), bd)
                tp_s[c, g, 0:C, :] = tp_s[c, g, 0:C, :] + xx[:C]
                tp_s[c, g, C:2 * C, :] = xx[C:]
            else:
                t = tp_s[c, g, 0:C, :]
                tp_s[c, g, 0:C, :] = t + _dot(mx(t), bd)

    for c, h in heads:
        g, j = divmod(h, grp)
        tb = tp_s[c, g, 0:C, j * C:(j + 1) * C] * narts[c][h:h + 1, :]
        tbe = tb * jnp.exp(gcts[c][N_HEADS + h:N_HEADS + h + 1, :])
        up_s[c, h] = _dot(mx(tb), mx(blk(v_ref, c, hsl(h))))
        wdq_s[c, h, 0:C, :] = _dot(mx(tbe), mx(blk(k_ref, c, hsl(h)))).astype(sdt)

    if carried:
        init = s0_ref[...] if has_init else jnp.zeros(st_s.shape, F32)
        st_s[...] = jnp.where(fresh, init, st_s[...])
        st_src = st_dst = st_s
    else:
        assert has_init
        st_src, st_dst = s0_ref, so_ref
    for cc in range(cpb):
        chains = [(s * cpb + cc, s, h) for s in range(spb) for h in range(N_HEADS)]
        rs = [_dot(mx(wdq_s[c, h]), mx(st_src[s, h])) for c, s, h in chains]
        us = [mx(up_s[c, h] - r[:C]) for (c, s, h), r in zip(chains, rs)]
        for (c, s, h), r, u in zip(chains, rs, us):
            o = r[C:] + _dot(mx(aqk_s[c, h]), u)
            o_dst[s, cc * C:(cc + 1) * C, hsl(h)] = o.astype(o_dst.dtype)
        for (c, s, h), u in zip(chains, us):
            dl = dl_s[c][:, N_HEADS + h:N_HEADS + h + 1]
            new = st_src[s, h] * dl + _dot(mx(kdt_s[c, h]), u)
            st_dst[s, h] = jnp.where(step < n_blocks, new, st_s[s, h]) if fused_back else new

    if carried:
        so_ref[...] = st_s[...]


def _delta(q, k, v, nar, *, n_seq, chunk, cpb, spb=1, s0=None, back=None, grouped=False,
           live_rows=None):
    rows = q.shape[0]
    seq_len = rows // n_seq
    per_seq = lambda a: a.reshape(n_seq, seq_len, a.shape[-1])
    tb = chunk * cpb
    blocks_per_seq = seq_len // tb
    n_blocks = (n_seq // spb) * blocks_per_seq
    has_init = s0 is not None
    fused_back = back is not None
    nch = spb * cpb
    cur = lambda s: jnp.minimum(s, n_blocks - 1)
    prev = lambda s: jnp.maximum(s - 1, 0)
    row_spec = lambda w, at: pl.BlockSpec(
        (spb, tb, w), lambda s: (at(s) // blocks_per_seq, at(s) % blocks_per_seq, 0))
    st_shape = (spb, N_HEADS, HEAD_DIM, HEAD_DIM)
    in_specs = [row_spec(D_MODEL, cur)] * 3 + [row_spec(NARROW, cur)]
    args = [per_seq(a) for a in (q, k, v, nar)]
    if has_init:
        in_specs.append(pl.BlockSpec(st_shape, lambda s: (cur(s) // blocks_per_seq, 0, 0, 0)))
        args.append(s0)
    sdt = BF16 if chunk % 16 == 0 else F32
    grp = min(N_HEADS, 256 // chunk)
    scratch = [pltpu.VMEM(st_shape, F32),
               pltpu.VMEM((nch, N_HEADS, chunk, HEAD_DIM), F32),
               pltpu.VMEM((nch, N_HEADS, 2 * chunk, HEAD_DIM), sdt),
               pltpu.VMEM((nch, N_HEADS, HEAD_DIM, chunk), sdt),
               pltpu.VMEM((nch, N_HEADS, chunk, chunk), sdt),
               pltpu.VMEM((nch, 1, NARROW), F32),
               pltpu.VMEM((nch, N_HEADS // grp, 2 * chunk, grp * chunk), F32)]
    if fused_back:
        x2d, zb, ma, sgb, consts = back
        in_specs += [row_spec(D_MODEL, prev)] * 4 + [_resident(c.shape) for c in consts]
        args += [per_seq(a) for a in (x2d, zb, ma, sgb)] + list(consts)
        out0 = (row_spec(D_MODEL, prev), jax.ShapeDtypeStruct((n_seq, seq_len, D_MODEL), F32))
        scratch += [pltpu.VMEM((spb, tb, D_MODEL), F32),
                    pltpu.VMEM((tb, D_MODEL), BF16), pltpu.VMEM((tb, D_MODEL), BF16)]
    else:
        out0 = (row_spec(D_MODEL, cur),
                jax.ShapeDtypeStruct((n_seq, seq_len, D_MODEL), q.dtype))
    out, state = pl.pallas_call(
        functools.partial(_delta_kernel, chunk, cpb, spb, n_blocks, blocks_per_seq, has_init,
                          fused_back, grouped, live_rows or chunk),
        grid=(n_blocks + 1 if fused_back else n_blocks,),
        in_specs=in_specs,
        out_specs=[out0[0],
                   pl.BlockSpec(st_shape, lambda s: (cur(s) // blocks_per_seq, 0, 0, 0))],
        out_shape=[out0[1], jax.ShapeDtypeStruct((n_seq, N_HEADS, HEAD_DIM, HEAD_DIM), F32)],
        scratch_shapes=scratch,
        compiler_params=pltpu.CompilerParams(
            dimension_semantics=("arbitrary",), vmem_limit_bytes=V7X_VMEM_LIMIT),
        name="delta",
    )(*args)
    return out.reshape(rows, D_MODEL), state


def _back_kernel(x_ref, o_ref, zb_ref, ma_ref, sgb_ref, onw_ref, wob_ref, wo_ref, fw_ref,
                 y_ref, on_s, m_s, grouped=False):
    for h in range(N_HEADS):
        hs = slice(h * HEAD_DIM, (h + 1) * HEAD_DIM)
        oh = o_ref[:, hs].astype(F32)
        ms = jnp.mean(oh * oh, axis=-1, keepdims=True)
        on = oh * lax.rsqrt(ms + EPS) * onw_ref[...]
        on_s[:, hs] = (on * zb_ref[:, hs].astype(F32)).astype(BF16)
    for c0 in range(0, D_MODEL, COL_CHUNK):
        c1 = c0 + COL_CHUNK
        yb = _dot(on_s[...], wob_ref[:, c0:c1])
        m = ma_ref[:, c0:c1].astype(F32) + sgb_ref[:, c0:c1].astype(F32) * yb
        m_s[:, c0:c1] = m.astype(BF16)
    m = m_s[...]
    if grouped:
        m = _dot(_group_perm(m.shape[0]), m).astype(BF16)
    hres = x_ref[...] + _dot(m, wo_ref[...])
    var = jnp.mean(hres * hres, axis=-1, keepdims=True)
    y_ref[...] = hres * lax.rsqrt(var + EPS) * fw_ref[...]


def _back(x, o, zb, ma, sgb, consts):
    bs, ts, _ = x.shape
    onw, wob, wo, fw = consts
    bt_spec = pl.BlockSpec((bs, D_MODEL), lambda t: (0, t))
    row_spec = pl.BlockSpec((bs, D_MODEL), lambda t: (t, 0))
    y = pl.pallas_call(
        _back_kernel,
        grid=(ts,),
        in_specs=[bt_spec, bt_spec] + [row_spec] * 3 + [
            _resident(onw.shape), _resident(wob.shape), _resident(wo.shape), _resident(fw.shape)],
        out_specs=bt_spec,
        out_shape=jax.ShapeDtypeStruct((bs, ts * D_MODEL), F32),
        scratch_shapes=[pltpu.VMEM((bs, D_MODEL), BF16), pltpu.VMEM((bs, D_MODEL), BF16)],
        compiler_params=pltpu.CompilerParams(
            dimension_semantics=("arbitrary",), vmem_limit_bytes=V7X_VMEM_LIMIT),
        name="back",
    )(x.reshape(bs, -1), o.reshape(bs, -1), zb, ma, sgb, onw, wob, wo, fw)
    return y.reshape(x.shape)


def kernel(x_prompt, x_sample, state_conv_a, state_conv_qkv, state_delta, w_in, conv_a_w,
           conv_b_w, a_log, dt_bias, onorm_w, w_out_a, w_out_b, w_o, norm_w, final_norm_w):
    assert w_in.shape[0] == 1, "single layer"
    bp, tp, _ = x_prompt.shape
    bs, ts, _ = x_sample.shape
    wm = w_in[0].astype(BF16)
    n_wide = OFF_ZB + D_MODEL
    wn = jnp.pad(wm[:, n_wide:n_wide + 2 * N_HEADS], ((0, 0), (0, NARROW - 2 * N_HEADS)))
    wt = jnp.concatenate([wm[:, n_wide + 2 * N_HEADS:], wn], axis=1)
    hp = jnp.zeros((8, NARROW), F32)
    hp = hp.at[0, N_HEADS:2 * N_HEADS].set(a_log[0]).at[1, N_HEADS:2 * N_HEADS].set(dt_bias[0])
    front_consts = (norm_w[0][None, :], wm, wt, conv_a_w[0], conv_b_w[0], hp,
                    w_out_a[0].astype(BF16))
    back_consts = (onorm_w[0][None, :], w_out_b[0].astype(BF16), w_o[0].astype(BF16),
                   final_norm_w[None, :])

    xp = x_prompt.reshape(bp * tp, D_MODEL)
    ma, q, k, v, zb, sgb, nar, sa, sq = _front(xp, front_consts, tm=256, shift=1, n_seq=bp)
    y_prompt, s_new = _delta(q, k, v, nar, n_seq=bp, chunk=PROMPT_CHUNK, cpb=4, spb=2, grouped=True,
                             back=(xp, zb, ma, sgb, back_consts))
    y_prompt = y_prompt.reshape(bp, tp, D_MODEL)
    new_conv_a_prompt = sa[None]
    new_conv_qkv_prompt = sq[None]
    new_delta_prompt = s_new[None]

    sta = jnp.transpose(state_conv_a[0], (1, 0, 2)).reshape((CONV_A_W - 1) * bs, D_MODEL)
    stq = jnp.transpose(state_conv_qkv[0], (1, 0, 2)).reshape((CONV_B_W - 1) * bs, QKV_W)
    ma, q, k, v, zb, sgb, nar, cha, pqkv = _front(
        x_sample, front_consts, tm=bs, shift=bs, n_seq=1, state=(sta, stq))
    sa = jnp.concatenate([sta, cha], axis=0)[ts * bs:]
    sq = jnp.concatenate([stq, pqkv], axis=0)[ts * bs:]
    tpad = 8

    def to_batch_major(a):
        a = jnp.transpose(a.reshape(ts, bs, a.shape[-1]), (1, 0, 2)).astype(F32)
        return jnp.pad(a, ((0, 0), (0, tpad - ts), (0, 0))).reshape(bs * tpad, a.shape[-1])

    o, s_new = _delta(to_batch_major(q), to_batch_major(k), to_batch_major(v),
                      to_batch_major(nar), n_seq=bs, chunk=tpad, cpb=1, spb=8, s0=state_delta[0],
                      live_rows=ts)
    y_sample = _back(x_sample, o.reshape(bs, tpad, D_MODEL), zb, ma, sgb, back_consts)
    new_conv_a_sample = jnp.transpose(sa.reshape(CONV_A_W - 1, bs, D_MODEL), (1, 0, 2))[None]
    new_conv_qkv_sample = jnp.transpose(sq.reshape(CONV_B_W - 1, bs, QKV_W), (1, 0, 2))[None]
    new_delta_sample = s_new[None]

    return (y_prompt, y_sample, new_conv_a_prompt, new_conv_qkv_prompt, new_delta_prompt,
            new_conv_a_sample, new_conv_qkv_sample, new_delta_sample)
```

```python
import functools
import math

import jax
import jax.numpy as jnp
from jax import lax
from jax.experimental import pallas as pl
from jax.experimental.pallas import tpu as pltpu

D_MODEL = 1024
N_HEADS = 8
HEAD_DIM = 128
QKV_W = 3 * D_MODEL
CONV_A_W = 3
CONV_B_W = 4
PROMPT_CHUNK = 64
EPS = 1e-6
NARROW = 128
OFF_B, OFF_C, OFF_H, OFF_Z, OFF_QKV, OFF_ZB, OFF_GA, OFF_GB, OFF_NARROW = (
    0, 1024, 2048, 3072, 4096, 7168, 8192, 9216, 10240)

V7X_VMEM_LIMIT = 56 * 1024 * 1024
COL_CHUNK = 256

F32 = jnp.float32
BF16 = jnp.bfloat16


def _dot(a, b):
    return jnp.dot(a, b, preferred_element_type=F32)


def _silu(x):
    return x * jax.nn.sigmoid(x)


GROUP = PROMPT_CHUNK


def _group_time(r):
    return ((r & 7) << 3) | ((r >> 3) & 7)


def _group_perm(n):
    r = lax.broadcasted_iota(jnp.int32, (n, n), 0)
    c = lax.broadcasted_iota(jnp.int32, (n, n), 1)
    return (c == ((r & ~(GROUP - 1)) | _group_time(r))).astype(BF16)


def _softplus(x):
    return jnp.maximum(x, 0.0) + jnp.log(1.0 + jnp.exp(-jnp.abs(x)))


def _front_kernel(tm, shift, tiles_per_seq, has_state, *refs):
    if has_state:
        (x_ref, nw_ref, wm_ref, wt_ref, caw_ref, cbw_ref, hp_ref, woa_ref, sta_ref, stq_ref,
         ma_ref, q_ref, k_ref, v_ref, zb_ref, sgb_ref, nar_ref, sa_ref, sq_ref,
         xn_s, exta_s, extb_s, pre_s) = refs
    else:
        (x_ref, xnext_ref, nw_ref, wm_ref, wt_ref, caw_ref, cbw_ref, hp_ref, woa_ref,
         ma_ref, q_ref, k_ref, v_ref, zb_ref, sgb_ref, nar_ref, sa_ref, sq_ref,
         xn_s, exta_s, extb_s, pre_s, xn_next_s) = refs
        sta_ref = stq_ref = None
    grouped = shift == 1
    na = (CONV_A_W - 1) * shift
    nb = (CONV_B_W - 1) * shift
    first = (pl.program_id(0) % tiles_per_seq) == 0

    def normed(ref):
        x = ref[...]
        var = jnp.mean(x * x, axis=-1, keepdims=True)
        xn = (x * lax.rsqrt(var + EPS) * nw_ref[...]).astype(BF16)
        return _dot(_group_perm(tm), xn).astype(BF16) if grouped else xn

    if has_state:
        xn_s[...] = normed(x_ref)
    else:
        @pl.when(pl.program_id(0) == 0)
        def _():
            xn_next_s[...] = normed(x_ref)

        xn_s[...] = xn_next_s[...]

    def proj(off, c0, width=COL_CHUNK):
        w_ref, lo = (wm_ref, off + c0) if off < OFF_GA else (wt_ref, off - OFF_GA + c0)
        return _dot(xn_s[...], w_ref[:, lo:lo + width])

    if grouped:
        assert not has_state

        @pl.when(pl.program_id(0) == 0)
        def _():
            exta_s[...] = jnp.zeros(exta_s.shape, F32)
            extb_s[...] = jnp.zeros(extb_s.shape, F32)

        for ext_s in (exta_s, extb_s):
            ext_s[...] = jnp.where(first, 0.0, ext_s[...])

        def causal_conv(ext_s, width, w_ref, cur, c0, c1):
            taps = width - 1
            sub0 = lax.broadcasted_iota(jnp.int32, (8, c1 - c0), 0) == 0
            prev_tail = [ext_s[8 * i:8 * i + 8, c0:c1] for i in range(taps)]
            shifted = [[] for _ in range(taps)]
            for g0 in range(0, tm, GROUP):
                vrow = lambda v: cur[g0 + 8 * v:g0 + 8 * v + 8]
                tail = [vrow(8 - taps + i) for i in range(taps)]
                wrap = [jnp.where(sub0, pltpu.roll(p, 1, 0), pltpu.roll(t, 1, 0))
                        for p, t in zip(prev_tail, tail)]
                for d in range(1, taps + 1):
                    shifted[d - 1] += [vrow(v - d) if v >= d else wrap[v - d + taps]
                                       for v in range(8)]
                prev_tail = tail
            for i in range(taps):
                ext_s[8 * i:8 * i + 8, c0:c1] = prev_tail[i]
            conv = cur * w_ref[width - 1:width, c0:c1]
            for d in range(1, taps + 1):
                conv = conv + (jnp.concatenate(shifted[d - 1], axis=0)
                               * w_ref[width - 1 - d:width - d, c0:c1])
            return conv

        def last_rows(cur, n):
            rows = [tm - GROUP + _group_time(t) for t in range(GROUP - n, GROUP)]
            return [cur[r:r + 1] for r in rows]
    else:
        ha = exta_s.shape[0] - tm
        hb = extb_s.shape[0] - tm

        @pl.when(pl.program_id(0) == 0)
        def _():
            exta_s[tm:tm + ha, :] = jnp.zeros((ha, D_MODEL), F32)
            extb_s[tm:tm + hb, :] = jnp.zeros((hb, QKV_W), F32)

        for ext_s, st_ref, hrows in ((exta_s, sta_ref, ha), (extb_s, stq_ref, hb)):
            for c0 in range(0, ext_s.shape[1], COL_CHUNK):
                cs = slice(c0, c0 + COL_CHUNK)
                init = st_ref[:, cs] if has_state else jnp.zeros((hrows, COL_CHUNK), F32)
                ext_s[0:hrows, cs] = jnp.where(first, init, ext_s[tm:tm + hrows, cs])

        def causal_conv(ext_s, width, w_ref, cur, c0, c1):
            hrows = ext_s.shape[0] - tm
            ext_s[hrows:hrows + tm, c0:c1] = cur
            conv = cur * w_ref[width - 1:width, c0:c1]
            for j in range(width - 1):
                d = (width - 1 - j) * shift
                conv = conv + ext_s[hrows - d:hrows - d + tm, c0:c1] * w_ref[j:j + 1, c0:c1]
            return conv

    def narrow_dots():
        return (proj(OFF_NARROW, 0, NARROW),)

    def narrow_epi(nar):
        lane = lax.broadcasted_iota(jnp.int32, nar.shape, 1)
        beta = jax.nn.sigmoid(nar)
        gdec = -jnp.exp(hp_ref[0:1, :]) * _softplus(nar + hp_ref[1:2, :])
        nar_ref[...] = jnp.where(lane < N_HEADS, beta, jnp.where(lane < 2 * N_HEADS, gdec, 0.0))

    def branch_a_dots(c0):
        return tuple(proj(off, c0) for off in (OFF_B, OFF_C, OFF_H, OFF_Z))

    def branch_a_epi(c0, pb, pc, ph, pz):
        c1 = c0 + COL_CHUNK
        ch = pc * ph
        if has_state:
            sa_ref[:, c0:c1] = ch
        elif grouped:
            for i, r in enumerate(last_rows(ch, na)):
                sa_ref[i:i + 1, c0:c1] = r
        conv = causal_conv(exta_s, CONV_A_W, caw_ref, ch, c0, c1)
        pre_s[:, c0:c1] = (_silu(pz) * pb * conv).astype(BF16)

    def gate_a_dots(c0):
        return (_dot(pre_s[...], woa_ref[:, c0:c0 + COL_CHUNK]), proj(OFF_GA, c0))

    def gate_a_epi(c0, ya, ga):
        ma_ref[:, c0:c0 + COL_CHUNK] = (jax.nn.sigmoid(ga) * ya).astype(ma_ref.dtype)

    def qkv_dots(e0):
        return (proj(OFF_QKV, e0),)

    def qkv_epi(e0, p):
        g, c0 = divmod(e0, D_MODEL)
        out_ref = (q_ref, k_ref, v_ref)[g]
        if has_state:
            sq_ref[:, e0:e0 + COL_CHUNK] = p
        elif grouped:
            for i, r in enumerate(last_rows(p, nb)):
                sq_ref[i:i + 1, e0:e0 + COL_CHUNK] = r
        s = _silu(causal_conv(extb_s, CONV_B_W, cbw_ref, p, e0, e0 + COL_CHUNK))
        if g == 2:
            out_ref[:, c0:c0 + COL_CHUNK] = s.astype(out_ref.dtype)
            return
        scale = HEAD_DIM ** -0.5 if g == 0 else 1.0
        for h0 in range(0, COL_CHUNK, HEAD_DIM):
            sh = s[:, h0:h0 + HEAD_DIM]
            ss = jnp.sum(sh * sh, axis=-1, keepdims=True)
            out_ref[:, c0 + h0:c0 + h0 + HEAD_DIM] = (
                sh * (lax.rsqrt(ss + EPS) * scale)).astype(out_ref.dtype)

    def act_dots(off, c0):
        return (proj(off, c0),)

    def zb_epi(c0, p):
        zb_ref[:, c0:c0 + COL_CHUNK] = _silu(p).astype(zb_ref.dtype)

    def sgb_epi(c0, p):
        sgb_ref[:, c0:c0 + COL_CHUNK] = jax.nn.sigmoid(p).astype(sgb_ref.dtype)

    part = functools.partial
    cols = list(range(0, D_MODEL, COL_CHUNK))
    n_col = len(cols)
    stages = [[(part(qkv_dots, e0), part(qkv_epi, e0))] for e0 in range(0, QKV_W, COL_CHUNK)]
    for n, c0 in enumerate(cols):
        stages[2 * n].append((part(branch_a_dots, c0), part(branch_a_epi, c0)))
        stages[2 * n + 1] += [(part(act_dots, OFF_ZB, c0), part(zb_epi, c0)),
                              (part(act_dots, OFF_GB, c0), part(sgb_epi, c0))]
        stages[2 * n_col + n].append((part(gate_a_dots, c0), part(gate_a_epi, c0)))
    stages[2 * n_col].append((narrow_dots, narrow_epi))
    assert len(stages) == 3 * n_col

    def run_dots(i):
        return [dots() for dots, _ in stages[i]]

    pending = run_dots(0)
    for i, stage in enumerate(stages):
        ready = pending
        if i + 1 < len(stages):
            pending = run_dots(i + 1)
        elif not has_state:
            xn_next_s[...] = normed(xnext_ref)
        for (_, epilogue), outs in zip(stage, ready):
            epilogue(*outs)
    if not has_state and not grouped:
        sa_ref[...] = exta_s[ha + tm - na:ha + tm, :]
        sq_ref[...] = extb_s[hb + tm - nb:hb + tm, :]


def _resident(shape):
    nd = len(shape)
    return pl.BlockSpec(shape, lambda *_: (0,) * nd, pipeline_mode=pl.Buffered(1))


def _front(x, consts, *, tm, shift, n_seq, state=None):
    if x.ndim == 3:
        assert x.shape[0] == tm == shift
        rows = x.shape[0] * x.shape[1]
        x = x.reshape(tm, -1)
        x_spec = pl.BlockSpec((tm, D_MODEL), lambda i: (0, i))
    else:
        rows = x.shape[0]
        x_spec = pl.BlockSpec((tm, D_MODEL), lambda i: (i, 0))
    n_tiles = rows // tm
    tiles_per_seq = n_tiles // n_seq
    na = (CONV_A_W - 1) * shift
    nb = (CONV_B_W - 1) * shift
    hdr_a = -(-na // 8) * 8
    hdr_b = -(-nb // 8) * 8
    has_state = state is not None
    nw, wm, wt, caw, cbw, hp, woa = consts

    row_spec = lambda w: pl.BlockSpec((tm, w), lambda i: (i, 0))
    in_specs = [x_spec, _resident(nw.shape), _resident(wm.shape), _resident(wt.shape),
                _resident(caw.shape), _resident(cbw.shape), _resident(hp.shape),
                _resident(woa.shape)]
    args = [x, nw, wm, wt, caw, cbw, hp, woa]
    if not has_state:
        in_specs.insert(1, pl.BlockSpec((tm, D_MODEL), lambda i: (jnp.minimum(i + 1, n_tiles - 1), 0)))
        args.insert(1, x)
    out_specs = [row_spec(D_MODEL)] * 6 + [row_spec(NARROW)]
    out_shape = [jax.ShapeDtypeStruct((rows, D_MODEL), BF16)] * 6 + [
        jax.ShapeDtypeStruct((rows, NARROW), F32)]
    if has_state:
        sta, stq = state
        assert sta.shape[0] == hdr_a == na and stq.shape[0] == hdr_b == nb
        in_specs += [_resident(sta.shape), _resident(stq.shape)]
        args += [sta, stq]
        out_specs += [row_spec(D_MODEL), row_spec(QKV_W)]
        out_shape += [jax.ShapeDtypeStruct((rows, D_MODEL), F32),
                      jax.ShapeDtypeStruct((rows, QKV_W), F32)]
    else:
        seq_spec = lambda r, w: pl.BlockSpec((None, r, w), lambda i: (i // tiles_per_seq, 0, 0))
        out_specs += [seq_spec(na, D_MODEL), seq_spec(nb, QKV_W)]
        out_shape += [jax.ShapeDtypeStruct((n_seq, na, D_MODEL), F32),
                      jax.ShapeDtypeStruct((n_seq, nb, QKV_W), F32)]
    if shift == 1:
        assert tm % GROUP == 0
        ext_rows = (8 * (CONV_A_W - 1), 8 * (CONV_B_W - 1))
    else:
        ext_rows = (hdr_a + tm, hdr_b + tm)
    scratch = [pltpu.VMEM((tm, D_MODEL), BF16),
               pltpu.VMEM((ext_rows[0], D_MODEL), F32),
               pltpu.VMEM((ext_rows[1], QKV_W), F32),
               pltpu.VMEM((tm, D_MODEL), BF16)]
    if not has_state:
        scratch.append(pltpu.VMEM((tm, D_MODEL), BF16))
    return pl.pallas_call(
        functools.partial(_front_kernel, tm, shift, tiles_per_seq, has_state),
        grid=(n_tiles,),
        in_specs=in_specs,
        out_specs=out_specs,
        out_shape=out_shape,
        scratch_shapes=scratch,
        compiler_params=pltpu.CompilerParams(
            dimension_semantics=("arbitrary",), vmem_limit_bytes=V7X_VMEM_LIMIT),
        name="front",
    )(*args)


def _delta_kernel(chunk, cpb, spb, n_blocks, blocks_per_seq, has_init, fused_back, grouped,
                  live_rows, *refs):
    refs = list(refs)
    q_ref, k_ref, v_ref, nar_ref = refs[:4]
    del refs[:4]
    s0_ref = refs.pop(0) if has_init else None
    if fused_back:
        back_in = refs[:8]
        del refs[:8]
    o_ref, so_ref, st_s, up_s, wdq_s, kdt_s, aqk_s, dl_s, tp_s = refs[:9]
    C = chunk
    nch = spb * cpb
    log_c = int(math.log2(C))
    levels = max(2, math.ceil(math.log2(live_rows)))
    carried = blocks_per_seq > 1 or cpb > 1
    grp = min(N_HEADS, 256 // C)
    W = grp * C
    sdt = wdq_s.dtype
    mx = lambda a: a.astype(BF16)
    step = pl.program_id(0)
    block = jnp.minimum(step, n_blocks - 1)
    fresh = (block % blocks_per_seq) == 0

    if fused_back:
        o_s, on_s, m_s = refs[9:]
        y_ref, o_dst = o_ref, o_s

        assert carried

        @pl.when(step == 0)
        def _():
            st_s[...] = jnp.zeros(st_s.shape, F32)
            o_s[...] = jnp.zeros(o_s.shape, F32)

        def run_back(s):
            _back_kernel(back_in[0].at[s], o_s.at[s], *[r.at[s] for r in back_in[1:4]],
                         *back_in[4:], y_ref.at[s], on_s, m_s, grouped=grouped)
    else:
        run_back = None
        o_dst = o_ref
        if carried:
            @pl.when(step == 0)
            def _():
                st_s[...] = jnp.zeros(st_s.shape, F32)

    row = lax.broadcasted_iota(jnp.int32, (C, C), 0)
    col = lax.broadcasted_iota(jnp.int32, (C, C), 1)
    row_t, col_t = (_group_time(row), _group_time(col)) if grouped else (row, col)
    causal = row_t >= col_t
    strict = row_t > col_t
    ltri = causal.astype(BF16)
    utri = (row_t <= col_t).astype(BF16)
    rw = lax.broadcasted_iota(jnp.int32, (W, W), 0)
    cw = lax.broadcasted_iota(jnp.int32, (W, W), 1)
    blockdiag = (rw >> log_c) == (cw >> log_c)
    ri = lax.broadcasted_iota(jnp.int32, (C, W), 0)
    ci = lax.broadcasted_iota(jnp.int32, (C, W), 1)
    eye_cat = (ri == (ci & (C - 1))).astype(F32)
    groups = [(c, g) for c in range(nch) for g in range(N_HEADS // grp)]

    def cumsum_t(a, tri, dims):
        hi = a.astype(BF16)
        lo = (a - hi.astype(F32)).astype(BF16)
        dn = (dims, ((), ()))
        if dims[0] == (1,):
            return (lax.dot_general(tri, hi, dn, preferred_element_type=F32)
                    + lax.dot_general(tri, lo, dn, preferred_element_type=F32))
        return (lax.dot_general(hi, tri, dn, preferred_element_type=F32)
                + lax.dot_general(lo, tri, dn, preferred_element_type=F32))

    heads = [(c, h) for c in range(nch) for h in range(N_HEADS)]
    hsl = lambda h: slice(h * HEAD_DIM, (h + 1) * HEAD_DIM)

    def blk(ref, c, cols=slice(None)):
        s, cc = divmod(c, cpb)
        return ref[s, cc * C:(cc + 1) * C, cols]

    ident = (row == col).astype(BF16)
    gcs = [cumsum_t(blk(nar_ref, c), ltri, ((1,), (0,))) for c in range(nch)]
    gcts = [cumsum_t(blk(nar_ref, c), utri, ((0,), (0,))) for c in range(nch)]
    narts = [cumsum_t(blk(nar_ref, c), ident, ((0,), (0,))) for c in range(nch)]
    a_kq = {}
    for c, h in heads:
        kx = mx(blk(k_ref, c, hsl(h)))
        a_kq[c, h] = lax.dot_general(
            jnp.concatenate([kx, mx(blk(q_ref, c, hsl(h)))], axis=0), kx,
            (((1,), (1,)), ((), ())), preferred_element_type=F32)
    for c in range(nch):
        dl_s[c] = jnp.exp(gcs[c][C - 1:C, :])
    for i, (c, h) in enumerate(heads):
        if run_back is not None and i % (len(heads) // spb) == 0:
            run_back(i // (len(heads) // spb))
        g, j = divmod(h, grp)
        gcw = jnp.broadcast_to(gcs[c][:, N_HEADS + h:N_HEADS + h + 1], (C, HEAD_DIM))
        betw = jnp.broadcast_to(blk(nar_ref, c, slice(h, h + 1)), (C, C))
        gcr = gcts[c][N_HEADS + h:N_HEADS + h + 1, :]
        decay = jnp.exp(jnp.where(causal, gcw[:, 0:C] - gcr, -jnp.inf))
        wdq_s[c, h, C:2 * C, :] = (blk(q_ref, c, hsl(h)).astype(F32) * jnp.exp(gcw)).astype(sdt)
        k_t = blk(k_ref, c, hsl(h)).astype(F32).T
        kdt_s[c, h] = (k_t * jnp.exp(gcr[:, C - 1:C] - gcr)).astype(sdt)
        a = a_kq[c, h]
        tp_s[c, g, C:2 * C, j * C:(j + 1) * C] = -jnp.where(strict, a[:C] * betw * decay, 0.0)
        aqk_s[c, h] = jnp.where(causal, a[C:] * decay, 0.0).astype(sdt)

    for c, g in groups:
        tp_s[c, g, 0:C, :] = eye_cat + tp_s[c, g, C:2 * C, :]
    for lv in range(levels):
        for c, g in groups:
            p = tp_s[c, g, C:2 * C, :]
            bd = mx(jnp.where(blockdiag, jnp.concatenate([p] * grp, axis=0), 0.0))
            if lv == 0:
                tp_s[c, g, C:2 * C, :] = _dot(mx(p), bd)
            elif lv < levels - 1:
                xx = _dot(mx(tp_s[c, g]), bd)
                tp_s[c, g, 0:C, :] = tp_s[c, g, 0:C, :] + xx[:C]
                tp_s[c, g, C:2 * C, :] = xx[C:]
            else:
                t = tp_s[c, g, 0:C, :]
                tp_s[c, g, 0:C, :] = t + _dot(mx(t), bd)

    for c, h in heads:
        g, j = divmod(h, grp)
        tb = tp_s[c, g, 0:C, j * C:(j + 1) * C] * narts[c][h:h + 1, :]
        tbe = tb * jnp.exp(gcts[c][N_HEADS + h:N_HEADS + h + 1, :])
        up_s[c, h] = _dot(mx(tb), mx(blk(v_ref, c, hsl(h))))
        wdq_s[c, h, 0:C, :] = _dot(mx(tbe), mx(blk(k_ref, c, hsl(h)))).astype(sdt)

    if carried:
        init = s0_ref[...] if has_init else jnp.zeros(st_s.shape, F32)
        st_s[...] = jnp.where(fresh, init, st_s[...])
        st_src = st_dst = st_s
    else:
        assert has_init
        st_src, st_dst = s0_ref, so_ref
    for cc in range(cpb):
        chains = [(s * cpb + cc, s, h) for s in range(spb) for h in range(N_HEADS)]
        rs = [_dot(mx(wdq_s[c, h]), mx(st_src[s, h])) for c, s, h in chains]
        us = [mx(up_s[c, h] - r[:C]) for (c, s, h), r in zip(chains, rs)]
        for (c, s, h), r, u in zip(chains, rs, us):
            o = r[C:] + _dot(mx(aqk_s[c, h]), u)
            o_dst[s, cc * C:(cc + 1) * C, hsl(h)] = o.astype(o_dst.dtype)
        for (c, s, h), u in zip(chains, us):
            dl = dl_s[c][:, N_HEADS + h:N_HEADS + h + 1]
            new = st_src[s, h] * dl + _dot(mx(kdt_s[c, h]), u)
            st_dst[s, h] = jnp.where(step < n_blocks, new, st_s[s, h]) if fused_back else new

    if carried:
        so_ref[...] = st_s[...]


def _delta(q, k, v, nar, *, n_seq, chunk, cpb, spb=1, s0=None, back=None, grouped=False,
           live_rows=None):
    rows = q.shape[0]
    seq_len = rows // n_seq
    per_seq = lambda a: a.reshape(n_seq, seq_len, a.shape[-1])
    tb = chunk * cpb
    blocks_per_seq = seq_len // tb
    n_blocks = (n_seq // spb) * blocks_per_seq
    has_init = s0 is not None
    fused_back = back is not None
    nch = spb * cpb
    cur = lambda s: jnp.minimum(s, n_blocks - 1)
    prev = lambda s: jnp.maximum(s - 1, 0)
    row_spec = lambda w, at: pl.BlockSpec(
        (spb, tb, w), lambda s: (at(s) // blocks_per_seq, at(s) % blocks_per_seq, 0))
    st_shape = (spb, N_HEADS, HEAD_DIM, HEAD_DIM)
    in_specs = [row_spec(D_MODEL, cur)] * 3 + [row_spec(NARROW, cur)]
    args = [per_seq(a) for a in (q, k, v, nar)]
    if has_init:
        in_specs.append(pl.BlockSpec(st_shape, lambda s: (cur(s) // blocks_per_seq, 0, 0, 0)))
        args.append(s0)
    sdt = BF16 if chunk % 16 == 0 else F32
    grp = min(N_HEADS, 256 // chunk)
    scratch = [pltpu.VMEM(st_shape, F32),
               pltpu.VMEM((nch, N_HEADS, chunk, HEAD_DIM), F32),
               pltpu.VMEM((nch, N_HEADS, 2 * chunk, HEAD_DIM), sdt),
               pltpu.VMEM((nch, N_HEADS, HEAD_DIM, chunk), sdt),
               pltpu.VMEM((nch, N_HEADS, chunk, chunk), sdt),
               pltpu.VMEM((nch, 1, NARROW), F32),
               pltpu.VMEM((nch, N_HEADS // grp, 2 * chunk, grp * chunk), F32)]
    if fused_back:
        x2d, zb, ma, sgb, consts = back
        in_specs += [row_spec(D_MODEL, prev)] * 4 + [_resident(c.shape) for c in consts]
        args += [per_seq(a) for a in (x2d, zb, ma, sgb)] + list(consts)
        out0 = (row_spec(D_MODEL, prev), jax.ShapeDtypeStruct((n_seq, seq_len, D_MODEL), F32))
        scratch += [pltpu.VMEM((spb, tb, D_MODEL), F32),
                    pltpu.VMEM((tb, D_MODEL), BF16), pltpu.VMEM((tb, D_MODEL), BF16)]
    else:
        out0 = (row_spec(D_MODEL, cur),
                jax.ShapeDtypeStruct((n_seq, seq_len, D_MODEL), q.dtype))
    out, state = pl.pallas_call(
        functools.partial(_delta_kernel, chunk, cpb, spb, n_blocks, blocks_per_seq, has_init,
                          fused_back, grouped, live_rows or chunk),
        grid=(n_blocks + 1 if fused_back else n_blocks,),
        in_specs=in_specs,
        out_specs=[out0[0],
                   pl.BlockSpec(st_shape, lambda s: (cur(s) // blocks_per_seq, 0, 0, 0))],
        out_shape=[out0[1], jax.ShapeDtypeStruct((n_seq, N_HEADS, HEAD_DIM, HEAD_DIM), F32)],
        scratch_shapes=scratch,
        compiler_params=pltpu.CompilerParams(
            dimension_semantics=("arbitrary",), vmem_limit_bytes=V7X_VMEM_LIMIT),
        name="delta",
    )(*args)
    return out.reshape(rows, D_MODEL), state


def _back_kernel(x_ref, o_ref, zb_ref, ma_ref, sgb_ref, onw_ref, wob_ref, wo_ref, fw_ref,
                 y_ref, on_s, m_s, grouped=False):
    for h in range(N_HEADS):
        hs = slice(h * HEAD_DIM, (h + 1) * HEAD_DIM)
        oh = o_ref[:, hs].astype(F32)
        ms = jnp.mean(oh * oh, axis=-1, keepdims=True)
        on = oh * lax.rsqrt(ms + EPS) * onw_ref[...]
        on_s[:, hs] = (on * zb_ref[:, hs].astype(F32)).astype(BF16)
    for c0 in range(0, D_MODEL, COL_CHUNK):
        c1 = c0 + COL_CHUNK
        yb = _dot(on_s[...], wob_ref[:, c0:c1])
        m = ma_ref[:, c0:c1].astype(F32) + sgb_ref[:, c0:c1].astype(F32) * yb
        m_s[:, c0:c1] = m.astype(BF16)
    m = m_s[...]
    if grouped:
        m = _dot(_group_perm(m.shape[0]), m).astype(BF16)
    hres = x_ref[...] + _dot(m, wo_ref[...])
    var = jnp.mean(hres * hres, axis=-1, keepdims=True)
    y_ref[...] = hres * lax.rsqrt(var + EPS) * fw_ref[...]


def _back(x, o, zb, ma, sgb, consts):
    bs, ts, _ = x.shape
    onw, wob, wo, fw = consts
    bt_spec = pl.BlockSpec((bs, D_MODEL), lambda t: (0, t))
    row_spec = pl.BlockSpec((bs, D_MODEL), lambda t: (t, 0))
    y = pl.pallas_call(
        _back_kernel,
        grid=(ts,),
        in_specs=[bt_spec, bt_spec] + [row_spec] * 3 + [
            _resident(onw.shape), _resident(wob.shape), _resident(wo.shape), _resident(fw.shape)],
        out_specs=bt_spec,
        out_shape=jax.ShapeDtypeStruct((bs, ts * D_MODEL), F32),
        scratch_shapes=[pltpu.VMEM((bs, D_MODEL), BF16), pltpu.VMEM((bs, D_MODEL), BF16)],
        compiler_params=pltpu.CompilerParams(
            dimension_semantics=("arbitrary",), vmem_limit_bytes=V7X_VMEM_LIMIT),
        name="back",
    )(x.reshape(bs, -1), o.reshape(bs, -1), zb, ma, sgb, onw, wob, wo, fw)
    return y.reshape(x.shape)


def kernel(x_prompt, x_sample, state_conv_a, state_conv_qkv, state_delta, w_in, conv_a_w,
           conv_b_w, a_log, dt_bias, onorm_w, w_out_a, w_out_b, w_o, norm_w, final_norm_w):
    assert w_in.shape[0] == 1, "single layer"
    bp, tp, _ = x_prompt.shape
    bs, ts, _ = x_sample.shape
    wm = w_in[0].astype(BF16)
    n_wide = OFF_ZB + D_MODEL
    wn = jnp.pad(wm[:, n_wide:n_wide + 2 * N_HEADS], ((0, 0), (0, NARROW - 2 * N_HEADS)))
    wt = jnp.concatenate([wm[:, n_wide + 2 * N_HEADS:], wn], axis=1)
    hp = jnp.zeros((8, NARROW), F32)
    hp = hp.at[0, N_HEADS:2 * N_HEADS].set(a_log[0]).at[1, N_HEADS:2 * N_HEADS].set(dt_bias[0])
    front_consts = (norm_w[0][None, :], wm, wt, conv_a_w[0], conv_b_w[0], hp,
                    w_out_a[0].astype(BF16))
    back_consts = (onorm_w[0][None, :], w_out_b[0].astype(BF16), w_o[0].astype(BF16),
                   final_norm_w[None, :])

    xp = x_prompt.reshape(bp * tp, D_MODEL)
    ma, q, k, v, zb, sgb, nar, sa, sq = _front(xp, front_consts, tm=256, shift=1, n_seq=bp)
    y_prompt, s_new = _delta(q, k, v, nar, n_seq=bp, chunk=PROMPT_CHUNK, cpb=4, spb=2, grouped=True,
                             back=(xp, zb, ma, sgb, back_consts))
    y_prompt = y_prompt.reshape(bp, tp, D_MODEL)
    new_conv_a_prompt = sa[None]
    new_conv_qkv_prompt = sq[None]
    new_delta_prompt = s_new[None]

    sta = jnp.transpose(state_conv_a[0], (1, 0, 2)).reshape((CONV_A_W - 1) * bs, D_MODEL)
    stq = jnp.transpose(state_conv_qkv[0], (1, 0, 2)).reshape((CONV_B_W - 1) * bs, QKV_W)
    ma, q, k, v, zb, sgb, nar, cha, pqkv = _front(
        x_sample, front_consts, tm=bs, shift=bs, n_seq=1, state=(sta, stq))
    sa = jnp.concatenate([sta, cha], axis=0)[ts * bs:]
    sq = jnp.concatenate([stq, pqkv], axis=0)[ts * bs:]
    tpad = 8

    def to_batch_major(a):
        a = jnp.transpose(a.reshape(ts, bs, a.shape[-1]), (1, 0, 2)).astype(F32)
        return jnp.pad(a, ((0, 0), (0, tpad - ts), (0, 0))).reshape(bs * tpad, a.shape[-1])

    o, s_new = _delta(to_batch_major(q), to_batch_major(k), to_batch_major(v),
                      to_batch_major(nar), n_seq=bs, chunk=tpad, cpb=1, spb=8, s0=state_delta[0],
                      live_rows=ts)
    y_sample = _back(x_sample, o.reshape(bs, tpad, D_MODEL), zb, ma, sgb, back_consts)
    new_conv_a_sample = jnp.transpose(sa.reshape(CONV_A_W - 1, bs, D_MODEL), (1, 0, 2))[None]
    new_conv_qkv_sample = jnp.transpose(sq.reshape(CONV_B_W - 1, bs, QKV_W), (1, 0, 2))[None]
    new_delta_sample = s_new[None]

    return (y_prompt, y_sample, new_conv_a_prompt, new_conv_qkv_prompt, new_delta_prompt,
            new_conv_a_sample, new_conv_qkv_sample, new_delta_sample)
```

```python
import functools
import math

import jax
import jax.numpy as jnp
from jax import lax
from jax.experimental import pallas as pl
from jax.experimental.pallas import tpu as pltpu

D_MODEL = 1024
N_HEADS = 8
HEAD_DIM = 128
QKV_W = 3 * D_MODEL
CONV_A_W = 3
CONV_B_W = 4
PROMPT_CHUNK = 64
EPS = 1e-6
NARROW = 128
OFF_B, OFF_C, OFF_H, OFF_Z, OFF_QKV, OFF_ZB, OFF_GA, OFF_GB, OFF_NARROW = (
    0, 1024, 2048, 3072, 4096, 7168, 8192, 9216, 10240)

V7X_VMEM_LIMIT = 56 * 1024 * 1024
FRONT_TILE = 256
DELTA_CHUNKS = 4
DELTA_SEQS = 2
SAMPLE_SEQS = 8
V7X_MXU_DIM = 256
SUBLANES = 8
COL_CHUNK = V7X_MXU_DIM

F32 = jnp.float32
BF16 = jnp.bfloat16


def _dot(a, b):
    return jnp.dot(a, b, preferred_element_type=F32)


def _silu(x):
    return x * jax.nn.sigmoid(x)


GROUP = PROMPT_CHUNK


def _group_time(r):
    return ((r & 7) << 3) | ((r >> 3) & 7)


def _group_perm(n):
    r = lax.broadcasted_iota(jnp.int32, (n, n), 0)
    c = lax.broadcasted_iota(jnp.int32, (n, n), 1)
    return (c == ((r & ~(GROUP - 1)) | _group_time(r))).astype(BF16)


def _softplus(x):
    return jnp.maximum(x, 0.0) + jnp.log(1.0 + jnp.exp(-jnp.abs(x)))


def _front_kernel(tm, shift, tiles_per_seq, has_state, *refs):
    if has_state:
        (x_ref, nw_ref, wm_ref, wt_ref, caw_ref, cbw_ref, hp_ref, woa_ref, sta_ref, stq_ref,
         ma_ref, q_ref, k_ref, v_ref, zb_ref, sgb_ref, nar_ref, sa_ref, sq_ref,
         xn_s, exta_s, extb_s, pre_s) = refs
    else:
        (x_ref, xnext_ref, nw_ref, wm_ref, wt_ref, caw_ref, cbw_ref, hp_ref, woa_ref,
         ma_ref, q_ref, k_ref, v_ref, zb_ref, sgb_ref, nar_ref, sa_ref, sq_ref,
         xn_s, exta_s, extb_s, pre_s, xn_next_s) = refs
        sta_ref = stq_ref = None
    grouped = shift == 1
    na = (CONV_A_W - 1) * shift
    nb = (CONV_B_W - 1) * shift
    first = (pl.program_id(0) % tiles_per_seq) == 0

    def normed(ref):
        x = ref[...]
        var = jnp.mean(x * x, axis=-1, keepdims=True)
        xn = (x * lax.rsqrt(var + EPS) * nw_ref[...]).astype(BF16)
        return _dot(_group_perm(tm), xn).astype(BF16) if grouped else xn

    if has_state:
        xn_s[...] = normed(x_ref)
    else:
        @pl.when(pl.program_id(0) == 0)
        def _():
            xn_next_s[...] = normed(x_ref)

        xn_s[...] = xn_next_s[...]

    def proj(off, c0, width=COL_CHUNK):
        w_ref, lo = (wm_ref, off + c0) if off < OFF_GA else (wt_ref, off - OFF_GA + c0)
        return _dot(xn_s[...], w_ref[:, lo:lo + width])

    if grouped:
        assert not has_state

        @pl.when(pl.program_id(0) == 0)
        def _():
            exta_s[...] = jnp.zeros(exta_s.shape, F32)
            extb_s[...] = jnp.zeros(extb_s.shape, F32)

        for ext_s in (exta_s, extb_s):
            ext_s[...] = jnp.where(first, 0.0, ext_s[...])

        def causal_conv(ext_s, width, w_ref, cur, c0, c1):
            taps = width - 1
            R = SUBLANES
            sub0 = lax.broadcasted_iota(jnp.int32, (R, c1 - c0), 0) == 0
            prev_tail = [ext_s[R * i:R * (i + 1), c0:c1] for i in range(taps)]
            shifted = [[] for _ in range(taps)]
            for g0 in range(0, tm, GROUP):
                vrow = lambda v: cur[g0 + R * v:g0 + R * (v + 1)]
                tail = [vrow(R - taps + i) for i in range(taps)]
                wrap = [jnp.where(sub0, pltpu.roll(p, 1, 0), pltpu.roll(t, 1, 0))
                        for p, t in zip(prev_tail, tail)]
                for d in range(1, taps + 1):
                    shifted[d - 1] += [vrow(v - d) if v >= d else wrap[v - d + taps]
                                       for v in range(R)]
                prev_tail = tail
            for i in range(taps):
                ext_s[R * i:R * (i + 1), c0:c1] = prev_tail[i]
            conv = cur * w_ref[width - 1:width, c0:c1]
            for d in range(1, taps + 1):
                conv = conv + (jnp.concatenate(shifted[d - 1], axis=0)
                               * w_ref[width - 1 - d:width - d, c0:c1])
            return conv

        def last_rows(cur, n):
            rows = [tm - GROUP + _group_time(t) for t in range(GROUP - n, GROUP)]
            return [cur[r:r + 1] for r in rows]
    else:
        ha = exta_s.shape[0] - tm
        hb = extb_s.shape[0] - tm

        @pl.when(pl.program_id(0) == 0)
        def _():
            exta_s[tm:tm + ha, :] = jnp.zeros((ha, D_MODEL), F32)
            extb_s[tm:tm + hb, :] = jnp.zeros((hb, QKV_W), F32)

        for ext_s, st_ref, hrows in ((exta_s, sta_ref, ha), (extb_s, stq_ref, hb)):
            for c0 in range(0, ext_s.shape[1], COL_CHUNK):
                cs = slice(c0, c0 + COL_CHUNK)
                init = st_ref[:, cs] if has_state else jnp.zeros((hrows, COL_CHUNK), F32)
                ext_s[0:hrows, cs] = jnp.where(first, init, ext_s[tm:tm + hrows, cs])

        def causal_conv(ext_s, width, w_ref, cur, c0, c1):
            hrows = ext_s.shape[0] - tm
            ext_s[hrows:hrows + tm, c0:c1] = cur
            conv = cur * w_ref[width - 1:width, c0:c1]
            for j in range(width - 1):
                d = (width - 1 - j) * shift
                conv = conv + ext_s[hrows - d:hrows - d + tm, c0:c1] * w_ref[j:j + 1, c0:c1]
            return conv

    def narrow_dots():
        return (proj(OFF_NARROW, 0, NARROW),)

    def narrow_epi(nar):
        lane = lax.broadcasted_iota(jnp.int32, nar.shape, 1)
        beta = jax.nn.sigmoid(nar)
        gdec = -jnp.exp(hp_ref[0:1, :]) * _softplus(nar + hp_ref[1:2, :])
        nar_ref[...] = jnp.where(lane < N_HEADS, beta, jnp.where(lane < 2 * N_HEADS, gdec, 0.0))

    def branch_a_dots(c0):
        return tuple(proj(off, c0) for off in (OFF_B, OFF_C, OFF_H, OFF_Z))

    def branch_a_epi(c0, pb, pc, ph, pz):
        c1 = c0 + COL_CHUNK
        ch = pc * ph
        if has_state:
            sa_ref[:, c0:c1] = ch
        elif grouped:
            for i, r in enumerate(last_rows(ch, na)):
                sa_ref[i:i + 1, c0:c1] = r
        conv = causal_conv(exta_s, CONV_A_W, caw_ref, ch, c0, c1)
        pre_s[:, c0:c1] = (_silu(pz) * pb * conv).astype(BF16)

    def gate_a_dots(c0):
        return (_dot(pre_s[...], woa_ref[:, c0:c0 + COL_CHUNK]), proj(OFF_GA, c0))

    def gate_a_epi(c0, ya, ga):
        ma_ref[:, c0:c0 + COL_CHUNK] = (jax.nn.sigmoid(ga) * ya).astype(ma_ref.dtype)

    def qkv_dots(e0):
        return (proj(OFF_QKV, e0),)

    def qkv_epi(e0, p):
        g, c0 = divmod(e0, D_MODEL)
        out_ref = (q_ref, k_ref, v_ref)[g]
        if has_state:
            sq_ref[:, e0:e0 + COL_CHUNK] = p
        elif grouped:
            for i, r in enumerate(last_rows(p, nb)):
                sq_ref[i:i + 1, e0:e0 + COL_CHUNK] = r
        s = _silu(causal_conv(extb_s, CONV_B_W, cbw_ref, p, e0, e0 + COL_CHUNK))
        if g == 2:
            out_ref[:, c0:c0 + COL_CHUNK] = s.astype(out_ref.dtype)
            return
        scale = HEAD_DIM ** -0.5 if g == 0 else 1.0
        for h0 in range(0, COL_CHUNK, HEAD_DIM):
            sh = s[:, h0:h0 + HEAD_DIM]
            ss = jnp.sum(sh * sh, axis=-1, keepdims=True)
            out_ref[:, c0 + h0:c0 + h0 + HEAD_DIM] = (
                sh * (lax.rsqrt(ss + EPS) * scale)).astype(out_ref.dtype)

    def act_dots(off, c0):
        return (proj(off, c0),)

    def zb_epi(c0, p):
        zb_ref[:, c0:c0 + COL_CHUNK] = _silu(p).astype(zb_ref.dtype)

    def sgb_epi(c0, p):
        sgb_ref[:, c0:c0 + COL_CHUNK] = jax.nn.sigmoid(p).astype(sgb_ref.dtype)

    part = functools.partial
    cols = list(range(0, D_MODEL, COL_CHUNK))
    n_col = len(cols)
    stages = [[(part(qkv_dots, e0), part(qkv_epi, e0))] for e0 in range(0, QKV_W, COL_CHUNK)]
    for n, c0 in enumerate(cols):
        stages[2 * n].append((part(branch_a_dots, c0), part(branch_a_epi, c0)))
        stages[2 * n + 1] += [(part(act_dots, OFF_ZB, c0), part(zb_epi, c0)),
                              (part(act_dots, OFF_GB, c0), part(sgb_epi, c0))]
        stages[2 * n_col + n].append((part(gate_a_dots, c0), part(gate_a_epi, c0)))
    stages[2 * n_col].append((narrow_dots, narrow_epi))
    assert len(stages) == 3 * n_col

    def run_dots(i):
        return [dots() for dots, _ in stages[i]]

    pending = run_dots(0)
    for i, stage in enumerate(stages):
        ready = pending
        if i + 1 < len(stages):
            pending = run_dots(i + 1)
        elif not has_state:
            xn_next_s[...] = normed(xnext_ref)
        for (_, epilogue), outs in zip(stage, ready):
            epilogue(*outs)
    if not has_state and not grouped:
        sa_ref[...] = exta_s[ha + tm - na:ha + tm, :]
        sq_ref[...] = extb_s[hb + tm - nb:hb + tm, :]


def _resident(shape):
    nd = len(shape)
    return pl.BlockSpec(shape, lambda *_: (0,) * nd, pipeline_mode=pl.Buffered(1))


def _front(x, consts, *, tm, shift, n_seq, state=None):
    if x.ndim == 3:
        assert x.shape[0] == tm == shift
        rows = x.shape[0] * x.shape[1]
        x = x.reshape(tm, -1)
        x_spec = pl.BlockSpec((tm, D_MODEL), lambda i: (0, i))
    else:
        rows = x.shape[0]
        x_spec = pl.BlockSpec((tm, D_MODEL), lambda i: (i, 0))
    n_tiles = rows // tm
    tiles_per_seq = n_tiles // n_seq
    na = (CONV_A_W - 1) * shift
    nb = (CONV_B_W - 1) * shift
    hdr_a = -(-na // SUBLANES) * SUBLANES
    hdr_b = -(-nb // SUBLANES) * SUBLANES
    has_state = state is not None
    nw, wm, wt, caw, cbw, hp, woa = consts

    row_spec = lambda w: pl.BlockSpec((tm, w), lambda i: (i, 0))
    in_specs = [x_spec, _resident(nw.shape), _resident(wm.shape), _resident(wt.shape),
                _resident(caw.shape), _resident(cbw.shape), _resident(hp.shape),
                _resident(woa.shape)]
    args = [x, nw, wm, wt, caw, cbw, hp, woa]
    if not has_state:
        in_specs.insert(1, pl.BlockSpec((tm, D_MODEL), lambda i: (jnp.minimum(i + 1, n_tiles - 1), 0)))
        args.insert(1, x)
    out_specs = [row_spec(D_MODEL)] * 6 + [row_spec(NARROW)]
    out_shape = [jax.ShapeDtypeStruct((rows, D_MODEL), BF16)] * 6 + [
        jax.ShapeDtypeStruct((rows, NARROW), F32)]
    if has_state:
        sta, stq = state
        assert sta.shape[0] == hdr_a == na and stq.shape[0] == hdr_b == nb
        in_specs += [_resident(sta.shape), _resident(stq.shape)]
        args += [sta, stq]
        out_specs += [row_spec(D_MODEL), row_spec(QKV_W)]
        out_shape += [jax.ShapeDtypeStruct((rows, D_MODEL), F32),
                      jax.ShapeDtypeStruct((rows, QKV_W), F32)]
    else:
        seq_spec = lambda r, w: pl.BlockSpec((None, r, w), lambda i: (i // tiles_per_seq, 0, 0))
        out_specs += [seq_spec(na, D_MODEL), seq_spec(nb, QKV_W)]
        out_shape += [jax.ShapeDtypeStruct((n_seq, na, D_MODEL), F32),
                      jax.ShapeDtypeStruct((n_seq, nb, QKV_W), F32)]
    if shift == 1:
        assert tm % GROUP == 0
        ext_rows = (SUBLANES * (CONV_A_W - 1), SUBLANES * (CONV_B_W - 1))
    else:
        ext_rows = (hdr_a + tm, hdr_b + tm)
    scratch = [pltpu.VMEM((tm, D_MODEL), BF16),
               pltpu.VMEM((ext_rows[0], D_MODEL), F32),
               pltpu.VMEM((ext_rows[1], QKV_W), F32),
               pltpu.VMEM((tm, D_MODEL), BF16)]
    if not has_state:
        scratch.append(pltpu.VMEM((tm, D_MODEL), BF16))
    return pl.pallas_call(
        functools.partial(_front_kernel, tm, shift, tiles_per_seq, has_state),
        grid=(n_tiles,),
        in_specs=in_specs,
        out_specs=out_specs,
        out_shape=out_shape,
        scratch_shapes=scratch,
        compiler_params=pltpu.CompilerParams(
            dimension_semantics=("arbitrary",), vmem_limit_bytes=V7X_VMEM_LIMIT),
        name="front",
    )(*args)


def _delta_kernel(chunk, cpb, spb, n_blocks, blocks_per_seq, has_init, fused_back, grouped,
                  live_rows, *refs):
    refs = list(refs)
    q_ref, k_ref, v_ref, nar_ref = refs[:4]
    del refs[:4]
    s0_ref = refs.pop(0) if has_init else None
    if fused_back:
        back_in = refs[:8]
        del refs[:8]
    o_ref, so_ref, st_s, up_s, wdq_s, kdt_s, aqk_s, dl_s, tp_s = refs[:9]
    C = chunk
    nch = spb * cpb
    log_c = int(math.log2(C))
    levels = max(2, math.ceil(math.log2(live_rows)))
    carried = blocks_per_seq > 1 or cpb > 1
    grp = min(N_HEADS, V7X_MXU_DIM // C)
    W = grp * C
    sdt = wdq_s.dtype
    mx = lambda a: a.astype(BF16)
    step = pl.program_id(0)
    block = jnp.minimum(step, n_blocks - 1)
    fresh = (block % blocks_per_seq) == 0

    if fused_back:
        o_s, on_s, m_s = refs[9:]
        y_ref, o_dst = o_ref, o_s

        assert carried

        @pl.when(step == 0)
        def _():
            st_s[...] = jnp.zeros(st_s.shape, F32)
            o_s[...] = jnp.zeros(o_s.shape, F32)

        def run_back(s):
            _back_kernel(back_in[0].at[s], o_s.at[s], *[r.at[s] for r in back_in[1:4]],
                         *back_in[4:], y_ref.at[s], on_s, m_s, grouped=grouped)
    else:
        run_back = None
        o_dst = o_ref
        if carried:
            @pl.when(step == 0)
            def _():
                st_s[...] = jnp.zeros(st_s.shape, F32)

    row = lax.broadcasted_iota(jnp.int32, (C, C), 0)
    col = lax.broadcasted_iota(jnp.int32, (C, C), 1)
    row_t, col_t = (_group_time(row), _group_time(col)) if grouped else (row, col)
    causal = row_t >= col_t
    strict = row_t > col_t
    ltri = causal.astype(BF16)
    utri = (row_t <= col_t).astype(BF16)
    rw = lax.broadcasted_iota(jnp.int32, (W, W), 0)
    cw = lax.broadcasted_iota(jnp.int32, (W, W), 1)
    blockdiag = (rw >> log_c) == (cw >> log_c)
    ri = lax.broadcasted_iota(jnp.int32, (C, W), 0)
    ci = lax.broadcasted_iota(jnp.int32, (C, W), 1)
    eye_cat = (ri == (ci & (C - 1))).astype(F32)
    groups = [(c, g) for c in range(nch) for g in range(N_HEADS // grp)]

    def cumsum_t(a, tri, dims):
        hi = a.astype(BF16)
        lo = (a - hi.astype(F32)).astype(BF16)
        dn = (dims, ((), ()))
        if dims[0] == (1,):
            return (lax.dot_general(tri, hi, dn, preferred_element_type=F32)
                    + lax.dot_general(tri, lo, dn, preferred_element_type=F32))
        return (lax.dot_general(hi, tri, dn, preferred_element_type=F32)
                + lax.dot_general(lo, tri, dn, preferred_element_type=F32))

    heads = [(c, h) for c in range(nch) for h in range(N_HEADS)]
    hsl = lambda h: slice(h * HEAD_DIM, (h + 1) * HEAD_DIM)

    def blk(ref, c, cols=slice(None)):
        s, cc = divmod(c, cpb)
        return ref[s, cc * C:(cc + 1) * C, cols]

    ident = (row == col).astype(BF16)
    gcs = [cumsum_t(blk(nar_ref, c), ltri, ((1,), (0,))) for c in range(nch)]
    gcts = [cumsum_t(blk(nar_ref, c), utri, ((0,), (0,))) for c in range(nch)]
    narts = [cumsum_t(blk(nar_ref, c), ident, ((0,), (0,))) for c in range(nch)]
    a_kq = {}
    for c, h in heads:
        kx = mx(blk(k_ref, c, hsl(h)))
        a_kq[c, h] = lax.dot_general(
            jnp.concatenate([kx, mx(blk(q_ref, c, hsl(h)))], axis=0), kx,
            (((1,), (1,)), ((), ())), preferred_element_type=F32)
    for c in range(nch):
        dl_s[c] = jnp.exp(gcs[c][C - 1:C, :])
    for i, (c, h) in enumerate(heads):
        if run_back is not None and i % (len(heads) // spb) == 0:
            run_back(i // (len(heads) // spb))
        g, j = divmod(h, grp)
        gcw = jnp.broadcast_to(gcs[c][:, N_HEADS + h:N_HEADS + h + 1], (C, HEAD_DIM))
        betw = jnp.broadcast_to(blk(nar_ref, c, slice(h, h + 1)), (C, C))
        gcr = gcts[c][N_HEADS + h:N_HEADS + h + 1, :]
        decay = jnp.exp(jnp.where(causal, gcw[:, 0:C] - gcr, -jnp.inf))
        wdq_s[c, h, C:2 * C, :] = (blk(q_ref, c, hsl(h)).astype(F32) * jnp.exp(gcw)).astype(sdt)
        k_t = blk(k_ref, c, hsl(h)).astype(F32).T
        kdt_s[c, h] = (k_t * jnp.exp(gcr[:, C - 1:C] - gcr)).astype(sdt)
        a = a_kq[c, h]
        tp_s[c, g, C:2 * C, j * C:(j + 1) * C] = -jnp.where(strict, a[:C] * betw * decay, 0.0)
        aqk_s[c, h] = jnp.where(causal, a[C:] * decay, 0.0).astype(sdt)

    for c, g in groups:
        tp_s[c, g, 0:C, :] = eye_cat + tp_s[c, g, C:2 * C, :]
    for lv in range(levels):
        for c, g in groups:
            p = tp_s[c, g, C:2 * C, :]
            bd = mx(jnp.where(blockdiag, jnp.concatenate([p] * grp, axis=0), 0.0))
            if lv == 0:
                tp_s[c, g, C:2 * C, :] = _dot(mx(p), bd)
            elif lv < levels - 1:
                xx = _dot(mx(tp_s[c, g]), bd)
                tp_s[c, g, 0:C, :] = tp_s[c, g, 0:C, :] + xx[:C]
                tp_s[c, g, C:2 * C, :] = xx[C:]
            else:
                t = tp_s[c, g, 0:C, :]
                tp_s[c, g, 0:C, :] = t + _dot(mx(t), bd)

    for c, h in heads:
        g, j = divmod(h, grp)
        tb = tp_s[c, g, 0:C, j * C:(j + 1) * C] * narts[c][h:h + 1, :]
        tbe = tb * jnp.exp(gcts[c][N_HEADS + h:N_HEADS + h + 1, :])
        up_s[c, h] = _dot(mx(tb), mx(blk(v_ref, c, hsl(h))))
        wdq_s[c, h, 0:C, :] = _dot(mx(tbe), mx(blk(k_ref, c, hsl(h)))).astype(sdt)

    if carried:
        init = s0_ref[...] if has_init else jnp.zeros(st_s.shape, F32)
        st_s[...] = jnp.where(fresh, init, st_s[...])
        st_src = st_dst = st_s
    else:
        assert has_init
        st_src, st_dst = s0_ref, so_ref
    for cc in range(cpb):
        chains = [(s * cpb + cc, s, h) for s in range(spb) for h in range(N_HEADS)]
        rs = [_dot(mx(wdq_s[c, h]), mx(st_src[s, h])) for c, s, h in chains]
        us = [mx(up_s[c, h] - r[:C]) for (c, s, h), r in zip(chains, rs)]
        for (c, s, h), r, u in zip(chains, rs, us):
            o = r[C:] + _dot(mx(aqk_s[c, h]), u)
            o_dst[s, cc * C:(cc + 1) * C, hsl(h)] = o.astype(o_dst.dtype)
        for (c, s, h), u in zip(chains, us):
            dl = dl_s[c][:, N_HEADS + h:N_HEADS + h + 1]
            new = st_src[s, h] * dl + _dot(mx(kdt_s[c, h]), u)
            st_dst[s, h] = jnp.where(step < n_blocks, new, st_s[s, h]) if fused_back else new

    if carried:
        so_ref[...] = st_s[...]


def _delta(q, k, v, nar, *, n_seq, chunk, cpb, spb=1, s0=None, back=None, grouped=False,
           live_rows=None):
    rows = q.shape[0]
    seq_len = rows // n_seq
    per_seq = lambda a: a.reshape(n_seq, seq_len, a.shape[-1])
    tb = chunk * cpb
    blocks_per_seq = seq_len // tb
    n_blocks = (n_seq // spb) * blocks_per_seq
    has_init = s0 is not None
    fused_back = back is not None
    nch = spb * cpb
    cur = lambda s: jnp.minimum(s, n_blocks - 1)
    prev = lambda s: jnp.maximum(s - 1, 0)
    row_spec = lambda w, at: pl.BlockSpec(
        (spb, tb, w), lambda s: (at(s) // blocks_per_seq, at(s) % blocks_per_seq, 0))
    st_shape = (spb, N_HEADS, HEAD_DIM, HEAD_DIM)
    in_specs = [row_spec(D_MODEL, cur)] * 3 + [row_spec(NARROW, cur)]
    args = [per_seq(a) for a in (q, k, v, nar)]
    if has_init:
        in_specs.append(pl.BlockSpec(st_shape, lambda s: (cur(s) // blocks_per_seq, 0, 0, 0)))
        args.append(s0)
    sdt = BF16 if chunk % 16 == 0 else F32
    grp = min(N_HEADS, V7X_MXU_DIM // chunk)
    scratch = [pltpu.VMEM(st_shape, F32),
               pltpu.VMEM((nch, N_HEADS, chunk, HEAD_DIM), F32),
               pltpu.VMEM((nch, N_HEADS, 2 * chunk, HEAD_DIM), sdt),
               pltpu.VMEM((nch, N_HEADS, HEAD_DIM, chunk), sdt),
               pltpu.VMEM((nch, N_HEADS, chunk, chunk), sdt),
               pltpu.VMEM((nch, 1, NARROW), F32),
               pltpu.VMEM((nch, N_HEADS // grp, 2 * chunk, grp * chunk), F32)]
    if fused_back:
        x2d, zb, ma, sgb, consts = back
        in_specs += [row_spec(D_MODEL, prev)] * 4 + [_resident(c.shape) for c in consts]
        args += [per_seq(a) for a in (x2d, zb, ma, sgb)] + list(consts)
        out0 = (row_spec(D_MODEL, prev), jax.ShapeDtypeStruct((n_seq, seq_len, D_MODEL), F32))
        scratch += [pltpu.VMEM((spb, tb, D_MODEL), F32),
                    pltpu.VMEM((tb, D_MODEL), BF16), pltpu.VMEM((tb, D_MODEL), BF16)]
    else:
        out0 = (row_spec(D_MODEL, cur),
                jax.ShapeDtypeStruct((n_seq, seq_len, D_MODEL), q.dtype))
    out, state = pl.pallas_call(
        functools.partial(_delta_kernel, chunk, cpb, spb, n_blocks, blocks_per_seq, has_init,
                          fused_back, grouped, live_rows or chunk),
        grid=(n_blocks + 1 if fused_back else n_blocks,),
        in_specs=in_specs,
        out_specs=[out0[0],
                   pl.BlockSpec(st_shape, lambda s: (cur(s) // blocks_per_seq, 0, 0, 0))],
        out_shape=[out0[1], jax.ShapeDtypeStruct((n_seq, N_HEADS, HEAD_DIM, HEAD_DIM), F32)],
        scratch_shapes=scratch,
        compiler_params=pltpu.CompilerParams(
            dimension_semantics=("arbitrary",), vmem_limit_bytes=V7X_VMEM_LIMIT),
        name="delta",
    )(*args)
    return out.reshape(rows, D_MODEL), state


def _back_kernel(x_ref, o_ref, zb_ref, ma_ref, sgb_ref, onw_ref, wob_ref, wo_ref, fw_ref,
                 y_ref, on_s, m_s, grouped=False):
    for h in range(N_HEADS):
        hs = slice(h * HEAD_DIM, (h + 1) * HEAD_DIM)
        oh = o_ref[:, hs].astype(F32)
        ms = jnp.mean(oh * oh, axis=-1, keepdims=True)
        on = oh * lax.rsqrt(ms + EPS) * onw_ref[...]
        on_s[:, hs] = (on * zb_ref[:, hs].astype(F32)).astype(BF16)
    for c0 in range(0, D_MODEL, COL_CHUNK):
        c1 = c0 + COL_CHUNK
        yb = _dot(on_s[...], wob_ref[:, c0:c1])
        m = ma_ref[:, c0:c1].astype(F32) + sgb_ref[:, c0:c1].astype(F32) * yb
        m_s[:, c0:c1] = m.astype(BF16)
    m = m_s[...]
    if grouped:
        m = _dot(_group_perm(m.shape[0]), m).astype(BF16)
    hres = x_ref[...] + _dot(m, wo_ref[...])
    var = jnp.mean(hres * hres, axis=-1, keepdims=True)
    y_ref[...] = hres * lax.rsqrt(var + EPS) * fw_ref[...]


def _back(x, o, zb, ma, sgb, consts):
    bs, ts, _ = x.shape
    onw, wob, wo, fw = consts
    bt_spec = pl.BlockSpec((bs, D_MODEL), lambda t: (0, t))
    row_spec = pl.BlockSpec((bs, D_MODEL), lambda t: (t, 0))
    y = pl.pallas_call(
        _back_kernel,
        grid=(ts,),
        in_specs=[bt_spec, bt_spec] + [row_spec] * 3 + [
            _resident(onw.shape), _resident(wob.shape), _resident(wo.shape), _resident(fw.shape)],
        out_specs=bt_spec,
        out_shape=jax.ShapeDtypeStruct((bs, ts * D_MODEL), F32),
        scratch_shapes=[pltpu.VMEM((bs, D_MODEL), BF16), pltpu.VMEM((bs, D_MODEL), BF16)],
        compiler_params=pltpu.CompilerParams(
            dimension_semantics=("arbitrary",), vmem_limit_bytes=V7X_VMEM_LIMIT),
        name="back",
    )(x.reshape(bs, -1), o.reshape(bs, -1), zb, ma, sgb, onw, wob, wo, fw)
    return y.reshape(x.shape)


def kernel(x_prompt, x_sample, state_conv_a, state_conv_qkv, state_delta, w_in, conv_a_w,
           conv_b_w, a_log, dt_bias, onorm_w, w_out_a, w_out_b, w_o, norm_w, final_norm_w):
    assert w_in.shape[0] == 1, "single layer"
    bp, tp, _ = x_prompt.shape
    bs, ts, _ = x_sample.shape
    wm = w_in[0].astype(BF16)
    n_wide = OFF_ZB + D_MODEL
    wn = jnp.pad(wm[:, n_wide:n_wide + 2 * N_HEADS], ((0, 0), (0, NARROW - 2 * N_HEADS)))
    wt = jnp.concatenate([wm[:, n_wide + 2 * N_HEADS:], wn], axis=1)
    hp = jnp.zeros((SUBLANES, NARROW), F32)
    hp = hp.at[0, N_HEADS:2 * N_HEADS].set(a_log[0]).at[1, N_HEADS:2 * N_HEADS].set(dt_bias[0])
    front_consts = (norm_w[0][None, :], wm, wt, conv_a_w[0], conv_b_w[0], hp,
                    w_out_a[0].astype(BF16))
    back_consts = (onorm_w[0][None, :], w_out_b[0].astype(BF16), w_o[0].astype(BF16),
                   final_norm_w[None, :])

    xp = x_prompt.reshape(bp * tp, D_MODEL)
    ma, q, k, v, zb, sgb, nar, sa, sq = _front(xp, front_consts, tm=FRONT_TILE, shift=1, n_seq=bp)
    y_prompt, s_new = _delta(q, k, v, nar, n_seq=bp, chunk=PROMPT_CHUNK, cpb=DELTA_CHUNKS,
                             spb=DELTA_SEQS, grouped=True, back=(xp, zb, ma, sgb, back_consts))
    y_prompt = y_prompt.reshape(bp, tp, D_MODEL)
    new_conv_a_prompt = sa[None]
    new_conv_qkv_prompt = sq[None]
    new_delta_prompt = s_new[None]

    sta = jnp.transpose(state_conv_a[0], (1, 0, 2)).reshape((CONV_A_W - 1) * bs, D_MODEL)
    stq = jnp.transpose(state_conv_qkv[0], (1, 0, 2)).reshape((CONV_B_W - 1) * bs, QKV_W)
    ma, q, k, v, zb, sgb, nar, cha, pqkv = _front(
        x_sample, front_consts, tm=bs, shift=bs, n_seq=1, state=(sta, stq))
    sa = jnp.concatenate([sta, cha], axis=0)[ts * bs:]
    sq = jnp.concatenate([stq, pqkv], axis=0)[ts * bs:]
    tpad = SUBLANES

    def to_batch_major(a):
        a = jnp.transpose(a.reshape(ts, bs, a.shape[-1]), (1, 0, 2)).astype(F32)
        return jnp.pad(a, ((0, 0), (0, tpad - ts), (0, 0))).reshape(bs * tpad, a.shape[-1])

    o, s_new = _delta(to_batch_major(q), to_batch_major(k), to_batch_major(v),
                      to_batch_major(nar), n_seq=bs, chunk=tpad, cpb=1, spb=SAMPLE_SEQS,
                      s0=state_delta[0], live_rows=ts)
    y_sample = _back(x_sample, o.reshape(bs, tpad, D_MODEL), zb, ma, sgb, back_consts)
    new_conv_a_sample = jnp.transpose(sa.reshape(CONV_A_W - 1, bs, D_MODEL), (1, 0, 2))[None]
    new_conv_qkv_sample = jnp.transpose(sq.reshape(CONV_B_W - 1, bs, QKV_W), (1, 0, 2))[None]
    new_delta_sample = s_new[None]

    return (y_prompt, y_sample, new_conv_a_prompt, new_conv_qkv_prompt, new_delta_prompt,
            new_conv_a_sample, new_conv_qkv_sample, new_delta_sample)
```

```python
import functools
import math

import jax
import jax.numpy as jnp
from jax import lax
from jax.experimental import pallas as pl
from jax.experimental.pallas import tpu as pltpu

D_MODEL = 1024
N_HEADS = 8
HEAD_DIM = 128
QKV_W = 3 * D_MODEL
CONV_A_W = 3
CONV_B_W = 4
PROMPT_CHUNK = 64
EPS = 1e-6
NARROW = 128
OFF_B, OFF_C, OFF_H, OFF_Z, OFF_QKV, OFF_ZB, OFF_GA, OFF_GB, OFF_NARROW = (
    0, 1024, 2048, 3072, 4096, 7168, 8192, 9216, 10240)

V7X_VMEM_LIMIT = 56 * 1024 * 1024
FRONT_TILE = 256
DELTA_CHUNKS = 4
DELTA_SEQS = 2
SAMPLE_SEQS = 8
V7X_MXU_DIM = 256
SUBLANES = 8
COL_CHUNK = V7X_MXU_DIM

F32 = jnp.float32
BF16 = jnp.bfloat16


def _dot(a, b):
    return jnp.dot(a, b, preferred_element_type=F32)


def _silu(x):
    return x * jax.nn.sigmoid(x)


GROUP = PROMPT_CHUNK


def _group_time(r):
    return ((r & 7) << 3) | ((r >> 3) & 7)


def _group_perm(n):
    r = lax.broadcasted_iota(jnp.int32, (n, n), 0)
    c = lax.broadcasted_iota(jnp.int32, (n, n), 1)
    return (c == ((r & ~(GROUP - 1)) | _group_time(r))).astype(BF16)


def _softplus(x):
    return jnp.maximum(x, 0.0) + jnp.log(1.0 + jnp.exp(-jnp.abs(x)))


def _front_kernel(tm, shift, tiles_per_seq, has_state, *refs):
    if has_state:
        (x_ref, nw_ref, wm_ref, wt_ref, caw_ref, cbw_ref, hp_ref, woa_ref, sta_ref, stq_ref,
         ma_ref, q_ref, k_ref, v_ref, zb_ref, sgb_ref, nar_ref, sa_ref, sq_ref,
         xn_s, exta_s, extb_s, pre_s) = refs
    else:
        (x_ref, xnext_ref, nw_ref, wm_ref, wt_ref, caw_ref, cbw_ref, hp_ref, woa_ref,
         ma_ref, q_ref, k_ref, v_ref, zb_ref, sgb_ref, nar_ref, sa_ref, sq_ref,
         xn_s, exta_s, extb_s, pre_s, xn_next_s) = refs
        sta_ref = stq_ref = None
    grouped = shift == 1
    na = (CONV_A_W - 1) * shift
    nb = (CONV_B_W - 1) * shift
    first = (pl.program_id(0) % tiles_per_seq) == 0

    def normed(ref):
        x = ref[...]
        var = jnp.mean(x * x, axis=-1, keepdims=True)
        xn = (x * lax.rsqrt(var + EPS) * nw_ref[...]).astype(BF16)
        return _dot(_group_perm(tm), xn).astype(BF16) if grouped else xn

    if has_state:
        xn_s[...] = normed(x_ref)
    else:
        @pl.when(pl.program_id(0) == 0)
        def _():
            xn_next_s[...] = normed(x_ref)

        xn_s[...] = xn_next_s[...]

    def proj(off, c0, width=COL_CHUNK):
        w_ref, lo = (wm_ref, off + c0) if off < OFF_GA else (wt_ref, off - OFF_GA + c0)
        return _dot(xn_s[...], w_ref[:, lo:lo + width])

    if grouped:
        assert not has_state

        @pl.when(pl.program_id(0) == 0)
        def _():
            exta_s[...] = jnp.zeros(exta_s.shape, F32)
            extb_s[...] = jnp.zeros(extb_s.shape, F32)

        for ext_s in (exta_s, extb_s):
            ext_s[...] = jnp.where(first, 0.0, ext_s[...])

        def causal_conv(ext_s, width, w_ref, cur, c0, c1):
            taps = width - 1
            R = SUBLANES
            sub0 = lax.broadcasted_iota(jnp.int32, (R, c1 - c0), 0) == 0
            prev_tail = [ext_s[R * i:R * (i + 1), c0:c1] for i in range(taps)]
            shifted = [[] for _ in range(taps)]
            for g0 in range(0, tm, GROUP):
                vrow = lambda v: cur[g0 + R * v:g0 + R * (v + 1)]
                tail = [vrow(R - taps + i) for i in range(taps)]
                wrap = [jnp.where(sub0, pltpu.roll(p, 1, 0), pltpu.roll(t, 1, 0))
                        for p, t in zip(prev_tail, tail)]
                for d in range(1, taps + 1):
                    shifted[d - 1] += [vrow(v - d) if v >= d else wrap[v - d + taps]
                                       for v in range(R)]
                prev_tail = tail
            for i in range(taps):
                ext_s[R * i:R * (i + 1), c0:c1] = prev_tail[i]
            conv = cur * w_ref[width - 1:width, c0:c1]
            for d in range(1, taps + 1):
                conv = conv + (jnp.concatenate(shifted[d - 1], axis=0)
                               * w_ref[width - 1 - d:width - d, c0:c1])
            return conv

        def last_rows(cur, n):
            rows = [tm - GROUP + _group_time(t) for t in range(GROUP - n, GROUP)]
            return [cur[r:r + 1] for r in rows]
    else:
        ha = exta_s.shape[0] - tm
        hb = extb_s.shape[0] - tm

        @pl.when(pl.program_id(0) == 0)
        def _():
            exta_s[tm:tm + ha, :] = jnp.zeros((ha, D_MODEL), F32)
            extb_s[tm:tm + hb, :] = jnp.zeros((hb, QKV_W), F32)

        for ext_s, st_ref, hrows in ((exta_s, sta_ref, ha), (extb_s, stq_ref, hb)):
            for c0 in range(0, ext_s.shape[1], COL_CHUNK):
                cs = slice(c0, c0 + COL_CHUNK)
                init = st_ref[:, cs] if has_state else jnp.zeros((hrows, COL_CHUNK), F32)
                ext_s[0:hrows, cs] = jnp.where(first, init, ext_s[tm:tm + hrows, cs])

        def causal_conv(ext_s, width, w_ref, cur, c0, c1):
            hrows = ext_s.shape[0] - tm
            ext_s[hrows:hrows + tm, c0:c1] = cur
            conv = cur * w_ref[width - 1:width, c0:c1]
            for j in range(width - 1):
                d = (width - 1 - j) * shift
                conv = conv + ext_s[hrows - d:hrows - d + tm, c0:c1] * w_ref[j:j + 1, c0:c1]
            return conv

    def narrow_dots():
        return (proj(OFF_NARROW, 0, NARROW),)

    def narrow_epi(nar):
        lane = lax.broadcasted_iota(jnp.int32, nar.shape, 1)
        beta = jax.nn.sigmoid(nar)
        gdec = -jnp.exp(hp_ref[0:1, :]) * _softplus(nar + hp_ref[1:2, :])
        nar_ref[...] = jnp.where(lane < N_HEADS, beta, jnp.where(lane < 2 * N_HEADS, gdec, 0.0))

    def branch_a_dots(c0):
        return tuple(proj(off, c0) for off in (OFF_B, OFF_C, OFF_H, OFF_Z))

    def branch_a_epi(c0, pb, pc, ph, pz):
        c1 = c0 + COL_CHUNK
        ch = pc * ph
        if has_state:
            sa_ref[:, c0:c1] = ch
        elif grouped:
            for i, r in enumerate(last_rows(ch, na)):
                sa_ref[i:i + 1, c0:c1] = r
        conv = causal_conv(exta_s, CONV_A_W, caw_ref, ch, c0, c1)
        pre_s[:, c0:c1] = (_silu(pz) * pb * conv).astype(BF16)

    def gate_a_dots(c0):
        return (_dot(pre_s[...], woa_ref[:, c0:c0 + COL_CHUNK]), proj(OFF_GA, c0))

    def gate_a_epi(c0, ya, ga):
        ma_ref[:, c0:c0 + COL_CHUNK] = (jax.nn.sigmoid(ga) * ya).astype(ma_ref.dtype)

    def qkv_dots(e0):
        return (proj(OFF_QKV, e0),)

    def qkv_epi(e0, p):
        g, c0 = divmod(e0, D_MODEL)
        out_ref = (q_ref, k_ref, v_ref)[g]
        if has_state:
            sq_ref[:, e0:e0 + COL_CHUNK] = p
        elif grouped:
            for i, r in enumerate(last_rows(p, nb)):
                sq_ref[i:i + 1, e0:e0 + COL_CHUNK] = r
        s = _silu(causal_conv(extb_s, CONV_B_W, cbw_ref, p, e0, e0 + COL_CHUNK))
        if g == 2:
            out_ref[:, c0:c0 + COL_CHUNK] = s.astype(out_ref.dtype)
            return
        scale = HEAD_DIM ** -0.5 if g == 0 else 1.0
        for h0 in range(0, COL_CHUNK, HEAD_DIM):
            sh = s[:, h0:h0 + HEAD_DIM]
            ss = jnp.sum(sh * sh, axis=-1, keepdims=True)
            out_ref[:, c0 + h0:c0 + h0 + HEAD_DIM] = (
                sh * (lax.rsqrt(ss + EPS) * scale)).astype(out_ref.dtype)

    def act_dots(off, c0):
        return (proj(off, c0),)

    def zb_epi(c0, p):
        zb_ref[:, c0:c0 + COL_CHUNK] = _silu(p).astype(zb_ref.dtype)

    def sgb_epi(c0, p):
        sgb_ref[:, c0:c0 + COL_CHUNK] = jax.nn.sigmoid(p).astype(sgb_ref.dtype)

    part = functools.partial
    cols = list(range(0, D_MODEL, COL_CHUNK))
    n_col = len(cols)
    stages = [[(part(qkv_dots, e0), part(qkv_epi, e0))] for e0 in range(0, QKV_W, COL_CHUNK)]
    for n, c0 in enumerate(cols):
        stages[2 * n].append((part(branch_a_dots, c0), part(branch_a_epi, c0)))
        stages[2 * n + 1] += [(part(act_dots, OFF_ZB, c0), part(zb_epi, c0)),
                              (part(act_dots, OFF_GB, c0), part(sgb_epi, c0))]
        stages[2 * n_col + n].append((part(gate_a_dots, c0), part(gate_a_epi, c0)))
    stages[2 * n_col].append((narrow_dots, narrow_epi))
    assert len(stages) == 3 * n_col

    def run_dots(i):
        return [dots() for dots, _ in stages[i]]

    pending = run_dots(0)
    for i, stage in enumerate(stages):
        ready = pending
        if i + 1 < len(stages):
            pending = run_dots(i + 1)
        elif not has_state:
            xn_next_s[...] = normed(xnext_ref)
        for (_, epilogue), outs in zip(stage, ready):
            epilogue(*outs)
    if not has_state and not grouped:
        sa_ref[...] = exta_s[ha + tm - na:ha + tm, :]
        sq_ref[...] = extb_s[hb + tm - nb:hb + tm, :]


def _resident(shape):
    nd = len(shape)
    return pl.BlockSpec(shape, lambda *_: (0,) * nd, pipeline_mode=pl.Buffered(1))


def _front(x, consts, *, tm, shift, n_seq, state=None):
    rows = x.shape[0]
    x_spec = pl.BlockSpec((tm, D_MODEL), lambda i: (i, 0))
    n_tiles = rows // tm
    tiles_per_seq = n_tiles // n_seq
    na = (CONV_A_W - 1) * shift
    nb = (CONV_B_W - 1) * shift
    hdr_a = -(-na // SUBLANES) * SUBLANES
    hdr_b = -(-nb // SUBLANES) * SUBLANES
    has_state = state is not None
    nw, wm, wt, caw, cbw, hp, woa = consts

    row_spec = lambda w: pl.BlockSpec((tm, w), lambda i: (i, 0))
    in_specs = [x_spec, _resident(nw.shape), _resident(wm.shape), _resident(wt.shape),
                _resident(caw.shape), _resident(cbw.shape), _resident(hp.shape),
                _resident(woa.shape)]
    args = [x, nw, wm, wt, caw, cbw, hp, woa]
    if not has_state:
        in_specs.insert(1, pl.BlockSpec((tm, D_MODEL), lambda i: (jnp.minimum(i + 1, n_tiles - 1), 0)))
        args.insert(1, x)
    out_specs = [row_spec(D_MODEL)] * 6 + [row_spec(NARROW)]
    out_shape = [jax.ShapeDtypeStruct((rows, D_MODEL), BF16)] * 6 + [
        jax.ShapeDtypeStruct((rows, NARROW), F32)]
    if has_state:
        sta, stq = state
        assert sta.shape[0] == hdr_a == na and stq.shape[0] == hdr_b == nb
        in_specs += [_resident(sta.shape), _resident(stq.shape)]
        args += [sta, stq]
        out_specs += [row_spec(D_MODEL), row_spec(QKV_W)]
        out_shape += [jax.ShapeDtypeStruct((rows, D_MODEL), F32),
                      jax.ShapeDtypeStruct((rows, QKV_W), F32)]
    else:
        seq_spec = lambda r, w: pl.BlockSpec((None, r, w), lambda i: (i // tiles_per_seq, 0, 0))
        out_specs += [seq_spec(na, D_MODEL), seq_spec(nb, QKV_W)]
        out_shape += [jax.ShapeDtypeStruct((n_seq, na, D_MODEL), F32),
                      jax.ShapeDtypeStruct((n_seq, nb, QKV_W), F32)]
    if shift == 1:
        assert tm % GROUP == 0
        ext_rows = (SUBLANES * (CONV_A_W - 1), SUBLANES * (CONV_B_W - 1))
    else:
        ext_rows = (hdr_a + tm, hdr_b + tm)
    scratch = [pltpu.VMEM((tm, D_MODEL), BF16),
               pltpu.VMEM((ext_rows[0], D_MODEL), F32),
               pltpu.VMEM((ext_rows[1], QKV_W), F32),
               pltpu.VMEM((tm, D_MODEL), BF16)]
    if not has_state:
        scratch.append(pltpu.VMEM((tm, D_MODEL), BF16))
    return pl.pallas_call(
        functools.partial(_front_kernel, tm, shift, tiles_per_seq, has_state),
        grid=(n_tiles,),
        in_specs=in_specs,
        out_specs=out_specs,
        out_shape=out_shape,
        scratch_shapes=scratch,
        compiler_params=pltpu.CompilerParams(
            dimension_semantics=("arbitrary",), vmem_limit_bytes=V7X_VMEM_LIMIT),
        name="front",
    )(*args)


def _delta_kernel(chunk, cpb, spb, n_blocks, blocks_per_seq, has_init, fused_back, grouped,
                  live_rows, *refs):
    refs = list(refs)
    q_ref, k_ref, v_ref, nar_ref = refs[:4]
    del refs[:4]
    s0_ref = refs.pop(0) if has_init else None
    if fused_back:
        back_in = refs[:8]
        del refs[:8]
    o_ref, so_ref, st_s, up_s, wdq_s, kdt_s, aqk_s, dl_s, tp_s = refs[:9]
    C = chunk
    nch = spb * cpb
    log_c = int(math.log2(C))
    levels = max(2, math.ceil(math.log2(live_rows)))
    carried = blocks_per_seq > 1 or cpb > 1
    grp = min(N_HEADS, V7X_MXU_DIM // C)
    W = grp * C
    sdt = wdq_s.dtype
    mx = lambda a: a.astype(BF16)
    step = pl.program_id(0)
    block = jnp.minimum(step, n_blocks - 1)
    fresh = (block % blocks_per_seq) == 0

    if fused_back:
        o_s, on_s, m_s = refs[9:]
        y_ref, o_dst = o_ref, o_s

        assert carried

        @pl.when(step == 0)
        def _():
            st_s[...] = jnp.zeros(st_s.shape, F32)
            o_s[...] = jnp.zeros(o_s.shape, F32)

        def run_back(s):
            _back_kernel(back_in[0].at[s], o_s.at[s], *[r.at[s] for r in back_in[1:4]],
                         *back_in[4:], y_ref.at[s], on_s, m_s, grouped=grouped)
    else:
        run_back = None
        o_dst = o_ref
        if carried:
            @pl.when(step == 0)
            def _():
                st_s[...] = jnp.zeros(st_s.shape, F32)

    row = lax.broadcasted_iota(jnp.int32, (C, C), 0)
    col = lax.broadcasted_iota(jnp.int32, (C, C), 1)
    row_t, col_t = (_group_time(row), _group_time(col)) if grouped else (row, col)
    causal = row_t >= col_t
    strict = row_t > col_t
    ltri = causal.astype(BF16)
    utri = (row_t <= col_t).astype(BF16)
    rw = lax.broadcasted_iota(jnp.int32, (W, W), 0)
    cw = lax.broadcasted_iota(jnp.int32, (W, W), 1)
    blockdiag = (rw >> log_c) == (cw >> log_c)
    ri = lax.broadcasted_iota(jnp.int32, (C, W), 0)
    ci = lax.broadcasted_iota(jnp.int32, (C, W), 1)
    eye_cat = (ri == (ci & (C - 1))).astype(F32)
    groups = [(c, g) for c in range(nch) for g in range(N_HEADS // grp)]

    def cumsum_t(a, tri, dims):
        hi = a.astype(BF16)
        lo = (a - hi.astype(F32)).astype(BF16)
        dn = (dims, ((), ()))
        if dims[0] == (1,):
            return (lax.dot_general(tri, hi, dn, preferred_element_type=F32)
                    + lax.dot_general(tri, lo, dn, preferred_element_type=F32))
        return (lax.dot_general(hi, tri, dn, preferred_element_type=F32)
                + lax.dot_general(lo, tri, dn, preferred_element_type=F32))

    heads = [(c, h) for c in range(nch) for h in range(N_HEADS)]
    hsl = lambda h: slice(h * HEAD_DIM, (h + 1) * HEAD_DIM)

    def blk(ref, c, cols=slice(None)):
        s, cc = divmod(c, cpb)
        return ref[s, cc * C:(cc + 1) * C, cols]

    ident = (row == col).astype(BF16)
    gcs = [cumsum_t(blk(nar_ref, c), ltri, ((1,), (0,))) for c in range(nch)]
    gcts = [cumsum_t(blk(nar_ref, c), utri, ((0,), (0,))) for c in range(nch)]
    narts = [cumsum_t(blk(nar_ref, c), ident, ((0,), (0,))) for c in range(nch)]
    a_kq = {}
    for c, h in heads:
        kx = mx(blk(k_ref, c, hsl(h)))
        a_kq[c, h] = lax.dot_general(
            jnp.concatenate([kx, mx(blk(q_ref, c, hsl(h)))], axis=0), kx,
            (((1,), (1,)), ((), ())), preferred_element_type=F32)
    for c in range(nch):
        dl_s[c] = jnp.exp(gcs[c][C - 1:C, :])
    for i, (c, h) in enumerate(heads):
        if run_back is not None and i % (len(heads) // spb) == 0:
            run_back(i // (len(heads) // spb))
        g, j = divmod(h, grp)
        gcw = jnp.broadcast_to(gcs[c][:, N_HEADS + h:N_HEADS + h + 1], (C, HEAD_DIM))
        betw = jnp.broadcast_to(blk(nar_ref, c, slice(h, h + 1)), (C, C))
        gcr = gcts[c][N_HEADS + h:N_HEADS + h + 1, :]
        decay = jnp.exp(jnp.where(causal, gcw[:, 0:C] - gcr, -jnp.inf))
        wdq_s[c, h, C:2 * C, :] = (blk(q_ref, c, hsl(h)).astype(F32) * jnp.exp(gcw)).astype(sdt)
        k_t = blk(k_ref, c, hsl(h)).astype(F32).T
        kdt_s[c, h] = (k_t * jnp.exp(gcr[:, C - 1:C] - gcr)).astype(sdt)
        a = a_kq[c, h]
        tp_s[c, g, C:2 * C, j * C:(j + 1) * C] = -jnp.where(strict, a[:C] * betw * decay, 0.0)
        aqk_s[c, h] = jnp.where(causal, a[C:] * decay, 0.0).astype(sdt)

    for c, g in groups:
        tp_s[c, g, 0:C, :] = eye_cat + tp_s[c, g, C:2 * C, :]
    for lv in range(levels):
        for c, g in groups:
            p = tp_s[c, g, C:2 * C, :]
            bd = mx(jnp.where(blockdiag, jnp.concatenate([p] * grp, axis=0), 0.0))
            if lv == 0:
                tp_s[c, g, C:2 * C, :] = _dot(mx(p), bd)
            elif lv < levels - 1:
                xx = _dot(mx(tp_s[c, g]), bd)
                tp_s[c, g, 0:C, :] = tp_s[c, g, 0:C, :] + xx[:C]
                tp_s[c, g, C:2 * C, :] = xx[C:]
            else:
                t = tp_s[c, g, 0:C, :]
                tp_s[c, g, 0:C, :] = t + _dot(mx(t), bd)

    for c, h in heads:
        g, j = divmod(h, grp)
        tb = tp_s[c, g, 0:C, j * C:(j + 1) * C] * narts[c][h:h + 1, :]
        tbe = tb * jnp.exp(gcts[c][N_HEADS + h:N_HEADS + h + 1, :])
        up_s[c, h] = _dot(mx(tb), mx(blk(v_ref, c, hsl(h))))
        wdq_s[c, h, 0:C, :] = _dot(mx(tbe), mx(blk(k_ref, c, hsl(h)))).astype(sdt)

    assert carried or has_init

    def state(s, h, cc):
        if not carried:
            return s0_ref[s, h]
        if cc > 0:
            return st_s[s, h]
        return jnp.where(fresh, s0_ref[s, h] if has_init else 0.0, st_s[s, h])

    for cc in range(cpb):
        chains = [(s * cpb + cc, s, h) for s in range(spb) for h in range(N_HEADS)]
        rs = [_dot(mx(wdq_s[c, h]), mx(state(s, h, cc))) for c, s, h in chains]
        us = [mx(up_s[c, h] - r[:C]) for (c, s, h), r in zip(chains, rs)]
        for (c, s, h), r, u in zip(chains, rs, us):
            o = r[C:] + _dot(mx(aqk_s[c, h]), u)
            o_dst[s, cc * C:(cc + 1) * C, hsl(h)] = o.astype(o_dst.dtype)
        for (c, s, h), u in zip(chains, us):
            dl = dl_s[c][:, N_HEADS + h:N_HEADS + h + 1]
            new = state(s, h, cc) * dl + _dot(mx(kdt_s[c, h]), u)
            if fused_back:
                new = jnp.where(step < n_blocks, new, st_s[s, h])
            if carried:
                st_s[s, h] = new
            if cc == cpb - 1:
                so_ref[s, h] = new


def _delta(q, k, v, nar, *, n_seq, chunk, cpb, spb=1, s0=None, back=None, grouped=False,
           live_rows=None):
    rows = q.shape[0]
    seq_len = rows // n_seq
    per_seq = lambda a: a.reshape(n_seq, seq_len, a.shape[-1])
    tb = chunk * cpb
    blocks_per_seq = seq_len // tb
    n_blocks = (n_seq // spb) * blocks_per_seq
    has_init = s0 is not None
    fused_back = back is not None
    nch = spb * cpb
    cur = lambda s: jnp.minimum(s, n_blocks - 1)
    prev = lambda s: jnp.maximum(s - 1, 0)
    row_spec = lambda w, at: pl.BlockSpec(
        (spb, tb, w), lambda s: (at(s) // blocks_per_seq, at(s) % blocks_per_seq, 0))
    st_shape = (spb, N_HEADS, HEAD_DIM, HEAD_DIM)
    in_specs = [row_spec(D_MODEL, cur)] * 3 + [row_spec(NARROW, cur)]
    args = [per_seq(a) for a in (q, k, v, nar)]
    if has_init:
        in_specs.append(pl.BlockSpec(st_shape, lambda s: (cur(s) // blocks_per_seq, 0, 0, 0)))
        args.append(s0)
    sdt = BF16 if chunk % 16 == 0 else F32
    grp = min(N_HEADS, V7X_MXU_DIM // chunk)
    scratch = [pltpu.VMEM(st_shape, F32),
               pltpu.VMEM((nch, N_HEADS, chunk, HEAD_DIM), F32),
               pltpu.VMEM((nch, N_HEADS, 2 * chunk, HEAD_DIM), sdt),
               pltpu.VMEM((nch, N_HEADS, HEAD_DIM, chunk), sdt),
               pltpu.VMEM((nch, N_HEADS, chunk, chunk), sdt),
               pltpu.VMEM((nch, 1, NARROW), F32),
               pltpu.VMEM((nch, N_HEADS // grp, 2 * chunk, grp * chunk), F32)]
    if fused_back:
        x2d, zb, ma, sgb, consts = back
        in_specs += [row_spec(D_MODEL, prev)] * 4 + [_resident(c.shape) for c in consts]
        args += [per_seq(a) for a in (x2d, zb, ma, sgb)] + list(consts)
        out0 = (row_spec(D_MODEL, prev), jax.ShapeDtypeStruct((n_seq, seq_len, D_MODEL), F32))
        scratch += [pltpu.VMEM((spb, tb, D_MODEL), F32),
                    pltpu.VMEM((tb, D_MODEL), BF16), pltpu.VMEM((tb, D_MODEL), BF16)]
    else:
        out0 = (row_spec(D_MODEL, cur),
                jax.ShapeDtypeStruct((n_seq, seq_len, D_MODEL), q.dtype))
    out, state = pl.pallas_call(
        functools.partial(_delta_kernel, chunk, cpb, spb, n_blocks, blocks_per_seq, has_init,
                          fused_back, grouped, live_rows or chunk),
        grid=(n_blocks + 1 if fused_back else n_blocks,),
        in_specs=in_specs,
        out_specs=[out0[0],
                   pl.BlockSpec(st_shape, lambda s: (cur(s) // blocks_per_seq, 0, 0, 0))],
        out_shape=[out0[1], jax.ShapeDtypeStruct((n_seq, N_HEADS, HEAD_DIM, HEAD_DIM), F32)],
        scratch_shapes=scratch,
        compiler_params=pltpu.CompilerParams(
            dimension_semantics=("arbitrary",), vmem_limit_bytes=V7X_VMEM_LIMIT),
        name="delta",
    )(*args)
    return out.reshape(rows, D_MODEL), state


def _back_kernel(x_ref, o_ref, zb_ref, ma_ref, sgb_ref, onw_ref, wob_ref, wo_ref, fw_ref,
                 y_ref, on_s, m_s, grouped=False):
    for h in range(N_HEADS):
        hs = slice(h * HEAD_DIM, (h + 1) * HEAD_DIM)
        oh = o_ref[:, hs].astype(F32)
        ms = jnp.mean(oh * oh, axis=-1, keepdims=True)
        on = oh * lax.rsqrt(ms + EPS) * onw_ref[...]
        on_s[:, hs] = (on * zb_ref[:, hs].astype(F32)).astype(BF16)
    for c0 in range(0, D_MODEL, COL_CHUNK):
        c1 = c0 + COL_CHUNK
        yb = _dot(on_s[...], wob_ref[:, c0:c1])
        m = ma_ref[:, c0:c1].astype(F32) + sgb_ref[:, c0:c1].astype(F32) * yb
        m_s[:, c0:c1] = m.astype(BF16)
    m = m_s[...]
    if grouped:
        m = _dot(_group_perm(m.shape[0]), m).astype(BF16)
    hres = x_ref[...] + _dot(m, wo_ref[...])
    var = jnp.mean(hres * hres, axis=-1, keepdims=True)
    y_ref[...] = hres * lax.rsqrt(var + EPS) * fw_ref[...]


def _back(x2d, o, zb, ma, sgb, consts, *, tm):
    rows = x2d.shape[0]
    onw, wob, wo, fw = consts
    row_spec = pl.BlockSpec((tm, D_MODEL), lambda i: (i, 0))
    return pl.pallas_call(
        _back_kernel,
        grid=(rows // tm,),
        in_specs=[row_spec] * 5 + [_resident(onw.shape), _resident(wob.shape),
                                   _resident(wo.shape), _resident(fw.shape)],
        out_specs=row_spec,
        out_shape=jax.ShapeDtypeStruct((rows, D_MODEL), F32),
        scratch_shapes=[pltpu.VMEM((tm, D_MODEL), BF16), pltpu.VMEM((tm, D_MODEL), BF16)],
        compiler_params=pltpu.CompilerParams(
            dimension_semantics=("arbitrary",), vmem_limit_bytes=V7X_VMEM_LIMIT),
        name="back",
    )(x2d, o, zb, ma, sgb, onw, wob, wo, fw)


def kernel(x_prompt, x_sample, state_conv_a, state_conv_qkv, state_delta, w_in, conv_a_w,
           conv_b_w, a_log, dt_bias, onorm_w, w_out_a, w_out_b, w_o, norm_w, final_norm_w):
    assert w_in.shape[0] == 1, "single layer"
    bp, tp, _ = x_prompt.shape
    bs, ts, _ = x_sample.shape
    wm = w_in[0].astype(BF16)
    n_wide = OFF_ZB + D_MODEL
    wn = jnp.pad(wm[:, n_wide:n_wide + 2 * N_HEADS], ((0, 0), (0, NARROW - 2 * N_HEADS)))
    wt = jnp.concatenate([wm[:, n_wide + 2 * N_HEADS:], wn], axis=1)
    hp = jnp.zeros((SUBLANES, NARROW), F32)
    hp = hp.at[0, N_HEADS:2 * N_HEADS].set(a_log[0]).at[1, N_HEADS:2 * N_HEADS].set(dt_bias[0])
    front_consts = (norm_w[0][None, :], wm, wt, conv_a_w[0], conv_b_w[0], hp,
                    w_out_a[0].astype(BF16))
    back_consts = (onorm_w[0][None, :], w_out_b[0].astype(BF16), w_o[0].astype(BF16),
                   final_norm_w[None, :])

    xp = x_prompt.reshape(bp * tp, D_MODEL)
    ma, q, k, v, zb, sgb, nar, sa, sq = _front(xp, front_consts, tm=FRONT_TILE, shift=1, n_seq=bp)
    y_prompt, s_new = _delta(q, k, v, nar, n_seq=bp, chunk=PROMPT_CHUNK, cpb=DELTA_CHUNKS,
                             spb=DELTA_SEQS, grouped=True, back=(xp, zb, ma, sgb, back_consts))
    y_prompt = y_prompt.reshape(bp, tp, D_MODEL)
    new_conv_a_prompt = sa[None]
    new_conv_qkv_prompt = sq[None]
    new_delta_prompt = s_new[None]

    xs = jnp.transpose(x_sample, (1, 0, 2)).reshape(ts * bs, D_MODEL)
    sta = jnp.transpose(state_conv_a[0], (1, 0, 2)).reshape((CONV_A_W - 1) * bs, D_MODEL)
    stq = jnp.transpose(state_conv_qkv[0], (1, 0, 2)).reshape((CONV_B_W - 1) * bs, QKV_W)
    ma, q, k, v, zb, sgb, nar, cha, pqkv = _front(
        xs, front_consts, tm=bs, shift=bs, n_seq=1, state=(sta, stq))
    sa = jnp.concatenate([sta, cha], axis=0)[ts * bs:]
    sq = jnp.concatenate([stq, pqkv], axis=0)[ts * bs:]
    tpad = SUBLANES

    def to_batch_major(a):
        a = jnp.transpose(a.reshape(ts, bs, a.shape[-1]), (1, 0, 2)).astype(F32)
        return jnp.pad(a, ((0, 0), (0, tpad - ts), (0, 0))).reshape(bs * tpad, a.shape[-1])

    o, s_new = _delta(to_batch_major(q), to_batch_major(k), to_batch_major(v),
                      to_batch_major(nar), n_seq=bs, chunk=tpad, cpb=1, spb=SAMPLE_SEQS,
                      s0=state_delta[0], live_rows=ts)
    o = jnp.transpose(o.reshape(bs, tpad, D_MODEL)[:, :ts], (1, 0, 2)).reshape(ts * bs, D_MODEL)
    ys = _back(xs, o, zb, ma, sgb, back_consts, tm=ts * bs)
    y_sample = jnp.transpose(ys.reshape(ts, bs, D_MODEL), (1, 0, 2))
    new_conv_a_sample = jnp.transpose(sa.reshape(CONV_A_W - 1, bs, D_MODEL), (1, 0, 2))[None]
    new_conv_qkv_sample = jnp.transpose(sq.reshape(CONV_B_W - 1, bs, QKV_W), (1, 0, 2))[None]
    new_delta_sample = s_new[None]

    return (y_prompt, y_sample, new_conv_a_prompt, new_conv_qkv_prompt, new_delta_prompt,
            new_conv_a_sample, new_conv_qkv_sample, new_delta_sample)
```

```python
import functools
import math

import jax
import jax.numpy as jnp
from jax import lax
from jax.experimental import pallas as pl
from jax.experimental.pallas import tpu as pltpu

D_MODEL = 1024
N_HEADS = 8
HEAD_DIM = 128
QKV_W = 3 * D_MODEL
CONV_A_W = 3
CONV_B_W = 4
PROMPT_CHUNK = 64
EPS = 1e-6
NARROW = 128
OFF_B, OFF_C, OFF_H, OFF_Z, OFF_QKV, OFF_ZB, OFF_GA, OFF_GB, OFF_NARROW = (
    0, 1024, 2048, 3072, 4096, 7168, 8192, 9216, 10240)

V7X_VMEM_LIMIT = 56 * 1024 * 1024
FRONT_TILE = 256
DELTA_CHUNKS = 4
DELTA_SEQS = 2
SAMPLE_SEQS = 8
V7X_MXU_DIM = 256
SUBLANES = 8
COL_CHUNK = V7X_MXU_DIM

F32 = jnp.float32
BF16 = jnp.bfloat16


def _dot(a, b):
    return jnp.dot(a, b, preferred_element_type=F32)


def _silu(x):
    return x * jax.nn.sigmoid(x)


GROUP = PROMPT_CHUNK


def _group_time(r):
    return ((r & 7) << 3) | ((r >> 3) & 7)


def _group_perm(n):
    r = lax.broadcasted_iota(jnp.int32, (n, n), 0)
    c = lax.broadcasted_iota(jnp.int32, (n, n), 1)
    return (c == ((r & ~(GROUP - 1)) | _group_time(r))).astype(BF16)


def _softplus(x):
    return jnp.maximum(x, 0.0) + jnp.log(1.0 + jnp.exp(-jnp.abs(x)))


def _front_body(tm, shift, tiles_per_seq, has_state, at_start, arrive, *refs):
    if has_state:
        (x_ref, nw_ref, wm_ref, wt_ref, caw_ref, cbw_ref, hp_ref, woa_ref, sta_ref, stq_ref,
         ma_ref, q_ref, k_ref, v_ref, zb_ref, sgb_ref, nar_ref, sa_ref, sq_ref,
         xn_s, exta_s, extb_s, pre_s) = refs
    else:
        (x_ref, xnext_ref, nw_ref, wm_ref, wt_ref, caw_ref, cbw_ref, hp_ref, woa_ref,
         ma_ref, q_ref, k_ref, v_ref, zb_ref, sgb_ref, nar_ref, sa_ref, sq_ref,
         xn_s, exta_s, extb_s, pre_s, xn_next_s) = refs
        sta_ref = stq_ref = None
    grouped = shift == 1
    na = (CONV_A_W - 1) * shift
    nb = (CONV_B_W - 1) * shift
    first = (pl.program_id(0) % tiles_per_seq) == 0

    def normed(ref):
        x = ref[...]
        var = jnp.mean(x * x, axis=-1, keepdims=True)
        xn = (x * lax.rsqrt(var + EPS) * nw_ref[...]).astype(BF16)
        return _dot(_group_perm(tm), xn).astype(BF16) if grouped else xn

    if has_state:
        xn_s[...] = normed(x_ref)
    else:
        if at_start:
            xn_next_s[...] = normed(x_ref)

        xn_s[...] = xn_next_s[...]

    def proj(off, c0, width=COL_CHUNK):
        if off >= OFF_GA:
            lo = off - OFF_GA + c0
            return _dot(xn_s[...], wt_ref[:, lo:lo + width])
        lo = off + c0
        assert width == COL_CHUNK and lo % COL_CHUNK == 0
        arrive(lo // COL_CHUNK)
        return _dot(xn_s[...], wm_ref[:, lo:lo + width])

    if grouped:
        assert not has_state

        if at_start:
            exta_s[...] = jnp.zeros(exta_s.shape, F32)
            extb_s[...] = jnp.zeros(extb_s.shape, F32)

        for ext_s in (exta_s, extb_s):
            ext_s[...] = jnp.where(first, 0.0, ext_s[...])

        def causal_conv(ext_s, width, w_ref, cur, c0, c1):
            taps = width - 1
            R = SUBLANES
            sub0 = lax.broadcasted_iota(jnp.int32, (R, c1 - c0), 0) == 0
            prev_tail = [ext_s[R * i:R * (i + 1), c0:c1] for i in range(taps)]
            shifted = [[] for _ in range(taps)]
            for g0 in range(0, tm, GROUP):
                vrow = lambda v: cur[g0 + R * v:g0 + R * (v + 1)]
                tail = [vrow(R - taps + i) for i in range(taps)]
                wrap = [jnp.where(sub0, pltpu.roll(p, 1, 0), pltpu.roll(t, 1, 0))
                        for p, t in zip(prev_tail, tail)]
                for d in range(1, taps + 1):
                    shifted[d - 1] += [vrow(v - d) if v >= d else wrap[v - d + taps]
                                       for v in range(R)]
                prev_tail = tail
            for i in range(taps):
                ext_s[R * i:R * (i + 1), c0:c1] = prev_tail[i]
            conv = cur * w_ref[width - 1:width, c0:c1]
            for d in range(1, taps + 1):
                conv = conv + (jnp.concatenate(shifted[d - 1], axis=0)
                               * w_ref[width - 1 - d:width - d, c0:c1])
            return conv

        def last_rows(cur, n):
            rows = [tm - GROUP + _group_time(t) for t in range(GROUP - n, GROUP)]
            return [cur[r:r + 1] for r in rows]
    else:
        ha = exta_s.shape[0] - tm
        hb = extb_s.shape[0] - tm

        if at_start:
            exta_s[tm:tm + ha, :] = jnp.zeros((ha, D_MODEL), F32)
            extb_s[tm:tm + hb, :] = jnp.zeros((hb, QKV_W), F32)

        for ext_s, st_ref, hrows in ((exta_s, sta_ref, ha), (extb_s, stq_ref, hb)):
            for c0 in range(0, ext_s.shape[1], COL_CHUNK):
                cs = slice(c0, c0 + COL_CHUNK)
                init = st_ref[:, cs] if has_state else jnp.zeros((hrows, COL_CHUNK), F32)
                ext_s[0:hrows, cs] = jnp.where(first, init, ext_s[tm:tm + hrows, cs])

        def causal_conv(ext_s, width, w_ref, cur, c0, c1):
            hrows = ext_s.shape[0] - tm
            ext_s[hrows:hrows + tm, c0:c1] = cur
            conv = cur * w_ref[width - 1:width, c0:c1]
            for j in range(width - 1):
                d = (width - 1 - j) * shift
                conv = conv + ext_s[hrows - d:hrows - d + tm, c0:c1] * w_ref[j:j + 1, c0:c1]
            return conv

    def narrow_dots():
        return (proj(OFF_NARROW, 0, NARROW),)

    def narrow_epi(nar):
        lane = lax.broadcasted_iota(jnp.int32, nar.shape, 1)
        beta = jax.nn.sigmoid(nar)
        gdec = -jnp.exp(hp_ref[0:1, :]) * _softplus(nar + hp_ref[1:2, :])
        nar_ref[...] = jnp.where(lane < N_HEADS, beta, jnp.where(lane < 2 * N_HEADS, gdec, 0.0))

    def branch_a_dots(c0):
        return tuple(proj(off, c0) for off in (OFF_B, OFF_C, OFF_H, OFF_Z))

    def branch_a_epi(c0, pb, pc, ph, pz):
        c1 = c0 + COL_CHUNK
        ch = pc * ph
        if has_state:
            sa_ref[:, c0:c1] = ch
        elif grouped:
            for i, r in enumerate(last_rows(ch, na)):
                sa_ref[i:i + 1, c0:c1] = r
        conv = causal_conv(exta_s, CONV_A_W, caw_ref, ch, c0, c1)
        pre_s[:, c0:c1] = (_silu(pz) * pb * conv).astype(BF16)

    def gate_a_dots(c0):
        return (_dot(pre_s[...], woa_ref[:, c0:c0 + COL_CHUNK]), proj(OFF_GA, c0))

    def gate_a_epi(c0, ya, ga):
        ma_ref[:, c0:c0 + COL_CHUNK] = (jax.nn.sigmoid(ga) * ya).astype(ma_ref.dtype)

    def qkv_dots(e0):
        return (proj(OFF_QKV, e0),)

    def qkv_epi(e0, p):
        g, c0 = divmod(e0, D_MODEL)
        out_ref = (q_ref, k_ref, v_ref)[g]
        if has_state:
            sq_ref[:, e0:e0 + COL_CHUNK] = p
        elif grouped:
            for i, r in enumerate(last_rows(p, nb)):
                sq_ref[i:i + 1, e0:e0 + COL_CHUNK] = r
        s = _silu(causal_conv(extb_s, CONV_B_W, cbw_ref, p, e0, e0 + COL_CHUNK))
        if g == 2:
            out_ref[:, c0:c0 + COL_CHUNK] = s.astype(out_ref.dtype)
            return
        scale = HEAD_DIM ** -0.5 if g == 0 else 1.0
        for h0 in range(0, COL_CHUNK, HEAD_DIM):
            sh = s[:, h0:h0 + HEAD_DIM]
            ss = jnp.sum(sh * sh, axis=-1, keepdims=True)
            out_ref[:, c0 + h0:c0 + h0 + HEAD_DIM] = (
                sh * (lax.rsqrt(ss + EPS) * scale)).astype(out_ref.dtype)

    def act_dots(off, c0):
        return (proj(off, c0),)

    def zb_epi(c0, p):
        zb_ref[:, c0:c0 + COL_CHUNK] = _silu(p).astype(zb_ref.dtype)

    def sgb_epi(c0, p):
        sgb_ref[:, c0:c0 + COL_CHUNK] = jax.nn.sigmoid(p).astype(sgb_ref.dtype)

    part = functools.partial
    cols = list(range(0, D_MODEL, COL_CHUNK))
    n_col = len(cols)
    stages = [[(part(qkv_dots, e0), part(qkv_epi, e0))] for e0 in range(0, QKV_W, COL_CHUNK)]
    for n, c0 in enumerate(cols):
        stages[2 * n].append((part(branch_a_dots, c0), part(branch_a_epi, c0)))
        stages[2 * n + 1] += [(part(act_dots, OFF_ZB, c0), part(zb_epi, c0)),
                              (part(act_dots, OFF_GB, c0), part(sgb_epi, c0))]
        stages[2 * n_col + n].append((part(gate_a_dots, c0), part(gate_a_epi, c0)))
    stages[2 * n_col].append((narrow_dots, narrow_epi))
    assert len(stages) == 3 * n_col

    def run_dots(i):
        return [dots() for dots, _ in stages[i]]

    pending = run_dots(0)
    for i, stage in enumerate(stages):
        ready = pending
        if i + 1 < len(stages):
            pending = run_dots(i + 1)
        elif not has_state:
            xn_next_s[...] = normed(xnext_ref)
        for (_, epilogue), outs in zip(stage, ready):
            epilogue(*outs)
    if not has_state and not grouped:
        sa_ref[...] = exta_s[ha + tm - na:ha + tm, :]
        sq_ref[...] = extb_s[hb + tm - nb:hb + tm, :]


def _front_kernel(tm, shift, tiles_per_seq, has_state, *refs):
    *refs, w_s, w_sem = refs
    wm_at = 2 if has_state else 3
    wm_hbm = refs[wm_at]
    refs[wm_at] = w_s
    body = functools.partial(_front_body, tm, shift, tiles_per_seq, has_state)
    step = pl.program_id(0)

    def chunk_copy(j):
        cols = pl.ds(j * COL_CHUNK, COL_CHUNK)
        return pltpu.make_async_copy(wm_hbm.at[:, cols], w_s.at[:, cols], w_sem.at[j])

    first_use = []

    @pl.when(step != 0)
    def _():
        body(False, lambda j: first_use.append(j) if j not in first_use else None, *refs)

    @pl.when(step == 0)
    def _():
        assert sorted(first_use) == list(range(w_s.shape[1] // COL_CHUNK))
        for j in first_use:
            chunk_copy(j).start()
        waited = set()

        def arrive(j):
            if j not in waited:
                waited.add(j)
                chunk_copy(j).wait()

        body(True, arrive, *refs)


def _resident(shape):
    nd = len(shape)
    return pl.BlockSpec(shape, lambda *_: (0,) * nd, pipeline_mode=pl.Buffered(1))


def _front(x, consts, *, tm, shift, n_seq, state=None):
    rows = x.shape[0]
    x_spec = pl.BlockSpec((tm, D_MODEL), lambda i: (i, 0))
    n_tiles = rows // tm
    tiles_per_seq = n_tiles // n_seq
    na = (CONV_A_W - 1) * shift
    nb = (CONV_B_W - 1) * shift
    hdr_a = -(-na // SUBLANES) * SUBLANES
    hdr_b = -(-nb // SUBLANES) * SUBLANES
    has_state = state is not None
    nw, wm, wt, caw, cbw, hp, woa = consts

    row_spec = lambda w: pl.BlockSpec((tm, w), lambda i: (i, 0))
    in_specs = [x_spec, _resident(nw.shape), pl.BlockSpec(memory_space=pl.ANY),
                _resident(wt.shape),
                _resident(caw.shape), _resident(cbw.shape), _resident(hp.shape),
                _resident(woa.shape)]
    args = [x, nw, wm, wt, caw, cbw, hp, woa]
    if not has_state:
        in_specs.insert(1, pl.BlockSpec((tm, D_MODEL), lambda i: (jnp.minimum(i + 1, n_tiles - 1), 0)))
        args.insert(1, x)
    out_specs = [row_spec(D_MODEL)] * 6 + [row_spec(NARROW)]
    out_shape = [jax.ShapeDtypeStruct((rows, D_MODEL), BF16)] * 6 + [
        jax.ShapeDtypeStruct((rows, NARROW), F32)]
    if has_state:
        sta, stq = state
        assert sta.shape[0] == hdr_a == na and stq.shape[0] == hdr_b == nb
        in_specs += [_resident(sta.shape), _resident(stq.shape)]
        args += [sta, stq]
        out_specs += [row_spec(D_MODEL), row_spec(QKV_W)]
        out_shape += [jax.ShapeDtypeStruct((rows, D_MODEL), F32),
                      jax.ShapeDtypeStruct((rows, QKV_W), F32)]
    else:
        seq_spec = lambda r, w: pl.BlockSpec((None, r, w), lambda i: (i // tiles_per_seq, 0, 0))
        out_specs += [seq_spec(na, D_MODEL), seq_spec(nb, QKV_W)]
        out_shape += [jax.ShapeDtypeStruct((n_seq, na, D_MODEL), F32),
                      jax.ShapeDtypeStruct((n_seq, nb, QKV_W), F32)]
    if shift == 1:
        assert tm % GROUP == 0
        ext_rows = (SUBLANES * (CONV_A_W - 1), SUBLANES * (CONV_B_W - 1))
    else:
        ext_rows = (hdr_a + tm, hdr_b + tm)
    scratch = [pltpu.VMEM((tm, D_MODEL), BF16),
               pltpu.VMEM((ext_rows[0], D_MODEL), F32),
               pltpu.VMEM((ext_rows[1], QKV_W), F32),
               pltpu.VMEM((tm, D_MODEL), BF16)]
    if not has_state:
        scratch.append(pltpu.VMEM((tm, D_MODEL), BF16))
    scratch += [pltpu.VMEM((D_MODEL, OFF_GA), BF16), pltpu.SemaphoreType.DMA((OFF_GA // COL_CHUNK,))]
    return pl.pallas_call(
        functools.partial(_front_kernel, tm, shift, tiles_per_seq, has_state),
        grid=(n_tiles,),
        in_specs=in_specs,
        out_specs=out_specs,
        out_shape=out_shape,
        scratch_shapes=scratch,
        compiler_params=pltpu.CompilerParams(
            dimension_semantics=("arbitrary",), vmem_limit_bytes=V7X_VMEM_LIMIT),
        name="front",
    )(*args)


def _delta_kernel(chunk, cpb, spb, n_blocks, blocks_per_seq, has_init, fused_back, grouped,
                  live_rows, *refs):
    refs = list(refs)
    q_ref, k_ref, v_ref, nar_ref = refs[:4]
    del refs[:4]
    s0_ref = refs.pop(0) if has_init else None
    if fused_back:
        back_in = refs[:8]
        del refs[:8]
    o_ref, so_ref, st_s, up_s, wdq_s, kdt_s, aqk_s, dl_s, tp_s = refs[:9]
    C = chunk
    nch = spb * cpb
    log_c = int(math.log2(C))
    levels = max(2, math.ceil(math.log2(live_rows)))
    carried = blocks_per_seq > 1 or cpb > 1
    grp = min(N_HEADS, V7X_MXU_DIM // C)
    W = grp * C
    sdt = wdq_s.dtype
    mx = lambda a: a.astype(BF16)
    step = pl.program_id(0)
    block = jnp.minimum(step, n_blocks - 1)
    fresh = (block % blocks_per_seq) == 0

    if fused_back:
        o_s, on_s, m_s = refs[9:]
        y_ref, o_dst = o_ref, o_s

        assert carried

        @pl.when(step == 0)
        def _():
            st_s[...] = jnp.zeros(st_s.shape, F32)
            o_s[...] = jnp.zeros(o_s.shape, F32)

        def run_back(s):
            _back_kernel(back_in[0].at[s], o_s.at[s], *[r.at[s] for r in back_in[1:4]],
                         *back_in[4:], y_ref.at[s], on_s, m_s, grouped=grouped)
    else:
        run_back = None
        o_dst = o_ref
        if carried:
            @pl.when(step == 0)
            def _():
                st_s[...] = jnp.zeros(st_s.shape, F32)

    row = lax.broadcasted_iota(jnp.int32, (C, C), 0)
    col = lax.broadcasted_iota(jnp.int32, (C, C), 1)
    row_t, col_t = (_group_time(row), _group_time(col)) if grouped else (row, col)
    causal = row_t >= col_t
    strict = row_t > col_t
    ltri = causal.astype(BF16)
    utri = (row_t <= col_t).astype(BF16)
    rw = lax.broadcasted_iota(jnp.int32, (W, W), 0)
    cw = lax.broadcasted_iota(jnp.int32, (W, W), 1)
    blockdiag = (rw >> log_c) == (cw >> log_c)
    ri = lax.broadcasted_iota(jnp.int32, (C, W), 0)
    ci = lax.broadcasted_iota(jnp.int32, (C, W), 1)
    eye_cat = (ri == (ci & (C - 1))).astype(F32)
    groups = [(c, g) for c in range(nch) for g in range(N_HEADS // grp)]

    def cumsum_t(a, tri, dims):
        hi = a.astype(BF16)
        lo = (a - hi.astype(F32)).astype(BF16)
        dn = (dims, ((), ()))
        if dims[0] == (1,):
            return (lax.dot_general(tri, hi, dn, preferred_element_type=F32)
                    + lax.dot_general(tri, lo, dn, preferred_element_type=F32))
        return (lax.dot_general(hi, tri, dn, preferred_element_type=F32)
                + lax.dot_general(lo, tri, dn, preferred_element_type=F32))

    heads = [(c, h) for c in range(nch) for h in range(N_HEADS)]
    hsl = lambda h: slice(h * HEAD_DIM, (h + 1) * HEAD_DIM)

    def blk(ref, c, cols=slice(None)):
        s, cc = divmod(c, cpb)
        return ref[s, cc * C:(cc + 1) * C, cols]

    ident = (row == col).astype(BF16)
    gcs = [cumsum_t(blk(nar_ref, c), ltri, ((1,), (0,))) for c in range(nch)]
    gcts = [cumsum_t(blk(nar_ref, c), utri, ((0,), (0,))) for c in range(nch)]
    narts = [cumsum_t(blk(nar_ref, c), ident, ((0,), (0,))) for c in range(nch)]
    a_kq = {}
    for c, h in heads:
        kx = mx(blk(k_ref, c, hsl(h)))
        a_kq[c, h] = lax.dot_general(
            jnp.concatenate([kx, mx(blk(q_ref, c, hsl(h)))], axis=0), kx,
            (((1,), (1,)), ((), ())), preferred_element_type=F32)
    for c in range(nch):
        dl_s[c] = jnp.exp(gcs[c][C - 1:C, :])
    for i, (c, h) in enumerate(heads):
        if run_back is not None and i % (len(heads) // spb) == 0:
            run_back(i // (len(heads) // spb))
        g, j = divmod(h, grp)
        gcw = jnp.broadcast_to(gcs[c][:, N_HEADS + h:N_HEADS + h + 1], (C, HEAD_DIM))
        betw = jnp.broadcast_to(blk(nar_ref, c, slice(h, h + 1)), (C, C))
        gcr = gcts[c][N_HEADS + h:N_HEADS + h + 1, :]
        decay = jnp.exp(jnp.where(causal, gcw[:, 0:C] - gcr, -jnp.inf))
        wdq_s[c, h, C:2 * C, :] = (blk(q_ref, c, hsl(h)).astype(F32) * jnp.exp(gcw)).astype(sdt)
        k_t = blk(k_ref, c, hsl(h)).astype(F32).T
        kdt_s[c, h] = (k_t * jnp.exp(gcr[:, C - 1:C] - gcr)).astype(sdt)
        a = a_kq[c, h]
        tp_s[c, g, C:2 * C, j * C:(j + 1) * C] = -jnp.where(strict, a[:C] * betw * decay, 0.0)
        aqk_s[c, h] = jnp.where(causal, a[C:] * decay, 0.0).astype(sdt)

    for c, g in groups:
        tp_s[c, g, 0:C, :] = eye_cat + tp_s[c, g, C:2 * C, :]
    for lv in range(levels):
        for c, g in groups:
            p = tp_s[c, g, C:2 * C, :]
            bd = mx(jnp.where(blockdiag, jnp.concatenate([p] * grp, axis=0), 0.0))
            if lv == 0:
                tp_s[c, g, C:2 * C, :] = _dot(mx(p), bd)
            elif lv < levels - 1:
                xx = _dot(mx(tp_s[c, g]), bd)
                tp_s[c, g, 0:C, :] = tp_s[c, g, 0:C, :] + xx[:C]
                tp_s[c, g, C:2 * C, :] = xx[C:]
            else:
                t = tp_s[c, g, 0:C, :]
                tp_s[c, g, 0:C, :] = t + _dot(mx(t), bd)

    for c, h in heads:
        g, j = divmod(h, grp)
        tb = tp_s[c, g, 0:C, j * C:(j + 1) * C] * narts[c][h:h + 1, :]
        tbe = tb * jnp.exp(gcts[c][N_HEADS + h:N_HEADS + h + 1, :])
        up_s[c, h] = _dot(mx(tb), mx(blk(v_ref, c, hsl(h))))
        wdq_s[c, h, 0:C, :] = _dot(mx(tbe), mx(blk(k_ref, c, hsl(h)))).astype(sdt)

    if carried:
        init = s0_ref[...] if has_init else jnp.zeros(st_s.shape, F32)
        st_s[...] = jnp.where(fresh, init, st_s[...])
        st_src = st_dst = st_s
    else:
        assert has_init
        st_src, st_dst = s0_ref, so_ref
    for cc in range(cpb):
        chains = [(s * cpb + cc, s, h) for s in range(spb) for h in range(N_HEADS)]
        rs = [_dot(mx(wdq_s[c, h]), mx(st_src[s, h])) for c, s, h in chains]
        us = [mx(up_s[c, h] - r[:C]) for (c, s, h), r in zip(chains, rs)]
        for (c, s, h), r, u in zip(chains, rs, us):
            o = r[C:] + _dot(mx(aqk_s[c, h]), u)
            o_dst[s, cc * C:(cc + 1) * C, hsl(h)] = o.astype(o_dst.dtype)
        for (c, s, h), u in zip(chains, us):
            dl = dl_s[c][:, N_HEADS + h:N_HEADS + h + 1]
            new = st_src[s, h] * dl + _dot(mx(kdt_s[c, h]), u)
            st_dst[s, h] = jnp.where(step < n_blocks, new, st_s[s, h]) if fused_back else new

    if carried:
        so_ref[...] = st_s[...]


def _delta(q, k, v, nar, *, n_seq, chunk, cpb, spb=1, s0=None, back=None, grouped=False,
           live_rows=None):
    rows = q.shape[0]
    seq_len = rows // n_seq
    per_seq = lambda a: a.reshape(n_seq, seq_len, a.shape[-1])
    tb = chunk * cpb
    blocks_per_seq = seq_len // tb
    n_blocks = (n_seq // spb) * blocks_per_seq
    has_init = s0 is not None
    fused_back = back is not None
    nch = spb * cpb
    cur = lambda s: jnp.minimum(s, n_blocks - 1)
    prev = lambda s: jnp.maximum(s - 1, 0)
    row_spec = lambda w, at: pl.BlockSpec(
        (spb, tb, w), lambda s: (at(s) // blocks_per_seq, at(s) % blocks_per_seq, 0))
    st_shape = (spb, N_HEADS, HEAD_DIM, HEAD_DIM)
    in_specs = [row_spec(D_MODEL, cur)] * 3 + [row_spec(NARROW, cur)]
    args = [per_seq(a) for a in (q, k, v, nar)]
    if has_init:
        in_specs.append(pl.BlockSpec(st_shape, lambda s: (cur(s) // blocks_per_seq, 0, 0, 0)))
        args.append(s0)
    sdt = BF16 if chunk % 16 == 0 else F32
    grp = min(N_HEADS, V7X_MXU_DIM // chunk)
    scratch = [pltpu.VMEM(st_shape, F32),
               pltpu.VMEM((nch, N_HEADS, chunk, HEAD_DIM), F32),
               pltpu.VMEM((nch, N_HEADS, 2 * chunk, HEAD_DIM), sdt),
               pltpu.VMEM((nch, N_HEADS, HEAD_DIM, chunk), sdt),
               pltpu.VMEM((nch, N_HEADS, chunk, chunk), sdt),
               pltpu.VMEM((nch, 1, NARROW), F32),
               pltpu.VMEM((nch, N_HEADS // grp, 2 * chunk, grp * chunk), F32)]
    if fused_back:
        x2d, zb, ma, sgb, consts = back
        in_specs += [row_spec(D_MODEL, prev)] * 4 + [_resident(c.shape) for c in consts]
        args += [per_seq(a) for a in (x2d, zb, ma, sgb)] + list(consts)
        out0 = (row_spec(D_MODEL, prev), jax.ShapeDtypeStruct((n_seq, seq_len, D_MODEL), F32))
        scratch += [pltpu.VMEM((spb, tb, D_MODEL), F32),
                    pltpu.VMEM((tb, D_MODEL), BF16), pltpu.VMEM((tb, D_MODEL), BF16)]
    else:
        out0 = (row_spec(D_MODEL, cur),
                jax.ShapeDtypeStruct((n_seq, seq_len, D_MODEL), q.dtype))
    out, state = pl.pallas_call(
        functools.partial(_delta_kernel, chunk, cpb, spb, n_blocks, blocks_per_seq, has_init,
                          fused_back, grouped, live_rows or chunk),
        grid=(n_blocks + 1 if fused_back else n_blocks,),
        in_specs=in_specs,
        out_specs=[out0[0],
                   pl.BlockSpec(st_shape, lambda s: (cur(s) // blocks_per_seq, 0, 0, 0))],
        out_shape=[out0[1], jax.ShapeDtypeStruct((n_seq, N_HEADS, HEAD_DIM, HEAD_DIM), F32)],
        scratch_shapes=scratch,
        compiler_params=pltpu.CompilerParams(
            dimension_semantics=("arbitrary",), vmem_limit_bytes=V7X_VMEM_LIMIT),
        name="delta",
    )(*args)
    return out.reshape(rows, D_MODEL), state


def _back_kernel(x_ref, o_ref, zb_ref, ma_ref, sgb_ref, onw_ref, wob_ref, wo_ref, fw_ref,
                 y_ref, on_s, m_s, grouped=False):
    for h in range(N_HEADS):
        hs = slice(h * HEAD_DIM, (h + 1) * HEAD_DIM)
        oh = o_ref[:, hs].astype(F32)
        ms = jnp.mean(oh * oh, axis=-1, keepdims=True)
        on = oh * lax.rsqrt(ms + EPS) * onw_ref[...]
        on_s[:, hs] = (on * zb_ref[:, hs].astype(F32)).astype(BF16)
    for c0 in range(0, D_MODEL, COL_CHUNK):
        c1 = c0 + COL_CHUNK
        yb = _dot(on_s[...], wob_ref[:, c0:c1])
        m = ma_ref[:, c0:c1].astype(F32) + sgb_ref[:, c0:c1].astype(F32) * yb
        m_s[:, c0:c1] = m.astype(BF16)
    m = m_s[...]
    if grouped:
        m = _dot(_group_perm(m.shape[0]), m).astype(BF16)
    hres = x_ref[...] + _dot(m, wo_ref[...])
    var = jnp.mean(hres * hres, axis=-1, keepdims=True)
    y_ref[...] = hres * lax.rsqrt(var + EPS) * fw_ref[...]


def _back(x2d, o, zb, ma, sgb, consts, *, tm):
    rows = x2d.shape[0]
    onw, wob, wo, fw = consts
    row_spec = pl.BlockSpec((tm, D_MODEL), lambda i: (i, 0))
    return pl.pallas_call(
        _back_kernel,
        grid=(rows // tm,),
        in_specs=[row_spec] * 5 + [_resident(onw.shape), _resident(wob.shape),
                                   _resident(wo.shape), _resident(fw.shape)],
        out_specs=row_spec,
        out_shape=jax.ShapeDtypeStruct((rows, D_MODEL), F32),
        scratch_shapes=[pltpu.VMEM((tm, D_MODEL), BF16), pltpu.VMEM((tm, D_MODEL), BF16)],
        compiler_params=pltpu.CompilerParams(
            dimension_semantics=("arbitrary",), vmem_limit_bytes=V7X_VMEM_LIMIT),
        name="back",
    )(x2d, o, zb, ma, sgb, onw, wob, wo, fw)


def kernel(x_prompt, x_sample, state_conv_a, state_conv_qkv, state_delta, w_in, conv_a_w,
           conv_b_w, a_log, dt_bias, onorm_w, w_out_a, w_out_b, w_o, norm_w, final_norm_w):
    assert w_in.shape[0] == 1, "single layer"
    bp, tp, _ = x_prompt.shape
    bs, ts, _ = x_sample.shape
    wm = w_in[0].astype(BF16)
    n_wide = OFF_ZB + D_MODEL
    wn = jnp.pad(wm[:, n_wide:n_wide + 2 * N_HEADS], ((0, 0), (0, NARROW - 2 * N_HEADS)))
    wt = jnp.concatenate([wm[:, n_wide + 2 * N_HEADS:], wn], axis=1)
    hp = jnp.zeros((SUBLANES, NARROW), F32)
    hp = hp.at[0, N_HEADS:2 * N_HEADS].set(a_log[0]).at[1, N_HEADS:2 * N_HEADS].set(dt_bias[0])
    front_consts = (norm_w[0][None, :], wm, wt, conv_a_w[0], conv_b_w[0], hp,
                    w_out_a[0].astype(BF16))
    back_consts = (onorm_w[0][None, :], w_out_b[0].astype(BF16), w_o[0].astype(BF16),
                   final_norm_w[None, :])

    xp = x_prompt.reshape(bp * tp, D_MODEL)
    ma, q, k, v, zb, sgb, nar, sa, sq = _front(xp, front_consts, tm=FRONT_TILE, shift=1, n_seq=bp)
    y_prompt, s_new = _delta(q, k, v, nar, n_seq=bp, chunk=PROMPT_CHUNK, cpb=DELTA_CHUNKS,
                             spb=DELTA_SEQS, grouped=True, back=(xp, zb, ma, sgb, back_consts))
    y_prompt = y_prompt.reshape(bp, tp, D_MODEL)
    new_conv_a_prompt = sa[None]
    new_conv_qkv_prompt = sq[None]
    new_delta_prompt = s_new[None]

    xs = jnp.transpose(x_sample, (1, 0, 2)).reshape(ts * bs, D_MODEL)
    sta = jnp.transpose(state_conv_a[0], (1, 0, 2)).reshape((CONV_A_W - 1) * bs, D_MODEL)
    stq = jnp.transpose(state_conv_qkv[0], (1, 0, 2)).reshape((CONV_B_W - 1) * bs, QKV_W)
    ma, q, k, v, zb, sgb, nar, cha, pqkv = _front(
        xs, front_consts, tm=bs, shift=bs, n_seq=1, state=(sta, stq))
    sa = jnp.concatenate([sta, cha], axis=0)[ts * bs:]
    sq = jnp.concatenate([stq, pqkv], axis=0)[ts * bs:]
    tpad = SUBLANES

    def to_batch_major(a):
        a = jnp.transpose(a.reshape(ts, bs, a.shape[-1]), (1, 0, 2)).astype(F32)
        return jnp.pad(a, ((0, 0), (0, tpad - ts), (0, 0))).reshape(bs * tpad, a.shape[-1])

    o, s_new = _delta(to_batch_major(q), to_batch_major(k), to_batch_major(v),
                      to_batch_major(nar), n_seq=bs, chunk=tpad, cpb=1, spb=SAMPLE_SEQS,
                      s0=state_delta[0], live_rows=ts)
    o = jnp.transpose(o.reshape(bs, tpad, D_MODEL)[:, :ts], (1, 0, 2)).reshape(ts * bs, D_MODEL)
    ys = _back(xs, o, zb, ma, sgb, back_consts, tm=ts * bs)
    y_sample = jnp.transpose(ys.reshape(ts, bs, D_MODEL), (1, 0, 2))
    new_conv_a_sample = jnp.transpose(sa.reshape(CONV_A_W - 1, bs, D_MODEL), (1, 0, 2))[None]
    new_conv_qkv_sample = jnp.transpose(sq.reshape(CONV_B_W - 1, bs, QKV_W), (1, 0, 2))[None]
    new_delta_sample = s_new[None]

    return (y_prompt, y_sample, new_conv_a_prompt, new_conv_qkv_prompt, new_delta_prompt,
            new_conv_a_sample, new_conv_qkv_sample, new_delta_sample)
```
